```python
import math
import jax, jax.numpy as jnp
from jax import lax
import numpy as np

D_MODEL = 4096
BATCH = 2
SEQ = 8192
DEPTH = 1
DEC_BATCH = 1
DEC_SEQ = 8192
PAST_LEN = 128

RMS_EPS = 1e-6
POOL_WIDTH = D_MODEL // 2
POOL_WINDOWS = (2, 4, 8, 16)
N_POOL_GROUPS = len(POOL_WINDOWS)
POOL_GROUP = POOL_WIDTH // N_POOL_GROUPS
N_HEADS = 16
NOPE_DIM = 128
ROPE_DIM = 64
V_DIM = 128
QK_DIM = NOPE_DIM + ROPE_DIM
Q_LORA = D_MODEL // 4
KV_LORA = D_MODEL // 8
MLA_WIDTH = N_HEADS * V_DIM
ROPE_THETA = 10000.0
Q_BLOCK = 128
D_FF = 11008
CONV_W = 3
OFF_POOL = 0
OFF_CQ = OFF_POOL + POOL_WIDTH
OFF_CKV = OFF_CQ + Q_LORA
OFF_KR = OFF_CKV + KV_LORA
OFF_GP = OFF_KR + ROPE_DIM
OFF_GM = OFF_GP + D_MODEL
IN_COLS = OFF_GM + D_MODEL

kernel_name = "gated_pool_mla_convglu_encoder"


def _rmsnorm(x, g):
    xf = x.astype(jnp.float32)
    y = xf * lax.rsqrt(jnp.mean(xf * xf, axis=-1, keepdims=True) + RMS_EPS)
    return (y * g.astype(jnp.float32)).astype(x.dtype)


def _rope_tables(S):
    inv = 1.0 / (ROPE_THETA ** (jnp.arange(0, ROPE_DIM, 2, dtype=jnp.float32) / ROPE_DIM))
    ang = jnp.arange(S, dtype=jnp.float32)[:, None] * inv[None, :]
    return jnp.cos(ang), jnp.sin(ang)


def _apply_rope(x, cos, sin):
    xf = x.astype(jnp.float32)
    x1, x2 = jnp.split(xf, 2, axis=-1)
    c = cos[None, :, None, :]
    s = sin[None, :, None, :]
    out = jnp.concatenate([x1 * c - x2 * s, x2 * c + x1 * s], axis=-1)
    return out.astype(x.dtype)


def _multiscale_pool(u):
    B, S, _ = u.shape
    ug = u.reshape(B, S, N_POOL_GROUPS, POOL_GROUP).astype(jnp.float32)
    cs = jnp.pad(jnp.cumsum(ug, axis=1), ((0, 0), (1, 0), (0, 0), (0, 0)))
    t = jnp.arange(S)
    means = []
    for gi, w in enumerate(POOL_WINDOWS):
        lo = jnp.clip(t - w // 2, 0, S)
        hi = jnp.clip(t + (w - w // 2), 0, S)
        csg = cs[:, :, gi]
        cnt = (hi - lo).astype(jnp.float32)[None, :, None]
        means.append((csg[:, hi] - csg[:, lo]) / cnt)
    mean = jnp.stack(means, axis=2)
    return (mean - ug).astype(u.dtype)


def _attention(q, k, v):
    B, S, H, Dq = q.shape
    nblk = S // Q_BLOCK
    scale = 1.0 / math.sqrt(Dq)
    qb = q.reshape(B, nblk, Q_BLOCK, H, Dq).transpose(1, 0, 2, 3, 4)

    def one_block(qi):
        s = jnp.einsum('bqhd,bkhd->bhqk', qi, k, preferred_element_type=jnp.float32) * scale
        p = jax.nn.softmax(s, axis=-1)
        return jnp.einsum('bhqk,bkhd->bqhd', p.astype(v.dtype), v)

    o = lax.map(one_block, qb)
    return o.transpose(1, 0, 2, 3, 4).reshape(B, S, H * V_DIM)


def _dwconv3(a, w, b):
    ap = jnp.pad(a, ((0, 0), (1, 1), (0, 0)))
    return ap[:, :-2] * w[0] + ap[:, 1:-1] * w[1] + ap[:, 2:] * w[2] + b


def _layer(x, norm_mix_gain, w_in, pool_w, pool_scale, q_a_norm_gain, w_uq, kv_a_norm_gain,
           w_ukv, q_norm_gain, k_norm_gain, w_branch_pool, w_branch_mla, w_o, norm_ffn_gain,
           w_up, conv_w, conv_b, w_down):
    B, S, _ = x.shape
    h = _rmsnorm(x, norm_mix_gain)
    z = h @ w_in
    u_pool = z[..., OFF_POOL:OFF_CQ]
    c_q = z[..., OFF_CQ:OFF_CKV]
    c_kv = z[..., OFF_CKV:OFF_KR]
    k_rope = z[..., OFF_KR:OFF_GP]
    g_pool = z[..., OFF_GP:OFF_GM]
    g_mla = z[..., OFF_GM:IN_COLS]

    pooled = _multiscale_pool(u_pool)
    a_out = jnp.einsum('bsgc,gcd->bsgd', pooled, pool_w).reshape(B, S, POOL_WIDTH) * pool_scale

    q = (_rmsnorm(c_q, q_a_norm_gain) @ w_uq).reshape(B, S, N_HEADS, QK_DIM)
    kv = (_rmsnorm(c_kv, kv_a_norm_gain) @ w_ukv).reshape(B, S, N_HEADS, NOPE_DIM + V_DIM)
    k_nope, v = kv[..., :NOPE_DIM], kv[..., NOPE_DIM:]
    k_r = jnp.broadcast_to(k_rope[:, :, None, :], (B, S, N_HEADS, ROPE_DIM))
    k = jnp.concatenate([k_nope, k_r], axis=-1)
    q = _rmsnorm(q, q_norm_gain)
    k = _rmsnorm(k, k_norm_gain)
    cos, sin = _rope_tables(S)
    q = jnp.concatenate([q[..., :NOPE_DIM], _apply_rope(q[..., NOPE_DIM:], cos, sin)], axis=-1)
    k = jnp.concatenate([k[..., :NOPE_DIM], _apply_rope(k[..., NOPE_DIM:], cos, sin)], axis=-1)
    b_out = _attention(q, k, v)

    m = jax.nn.sigmoid(g_pool) * (a_out @ w_branch_pool) + jax.nn.sigmoid(g_mla) * (b_out @ w_branch_mla)
    x = x + m @ w_o

    h2 = _rmsnorm(x, norm_ffn_gain)
    up = _dwconv3(h2 @ w_up, conv_w, conv_b)
    gate, val = up[..., :D_FF], up[..., D_FF:]
    return x + (jax.nn.silu(gate) * val) @ w_down


def _trunk(x, norm_mix_gain, w_in, pool_w, pool_scale, q_a_norm_gain, w_uq, kv_a_norm_gain,
           w_ukv, q_norm_gain, k_norm_gain, w_branch_pool, w_branch_mla, w_o, norm_ffn_gain,
           w_up, conv_w, conv_b, w_down):
    for l in range(DEPTH):
        x = _layer(x, norm_mix_gain[l], w_in[l], pool_w[l], pool_scale[l], q_a_norm_gain[l],
                   w_uq[l], kv_a_norm_gain[l], w_ukv[l], q_norm_gain[l], k_norm_gain[l],
                   w_branch_pool[l], w_branch_mla[l], w_o[l], norm_ffn_gain[l], w_up[l],
                   conv_w[l], conv_b[l], w_down[l])
    return x


def setup_inputs(seed: int = 0) -> dict:
    key = jax.random.key(seed)
    ks = jax.random.split(key, 20)
    f32 = jnp.float32
    L = DEPTH

    def nrm(k, shape, fan_in):
        return jax.random.normal(k, shape, f32) * (fan_in ** -0.5)

    def gain(k, n):
        return 1.0 + 0.02 * jax.random.normal(k, (L, n), f32)

    return {
        "x_prompt": jax.random.normal(ks[0], (BATCH, SEQ, D_MODEL), f32),
        "x_sample": jax.random.normal(ks[1], (DEC_BATCH, DEC_SEQ, D_MODEL), f32),
        "norm_mix_gain": gain(ks[2], D_MODEL),
        "w_in": nrm(ks[3], (L, D_MODEL, IN_COLS), D_MODEL),
        "pool_w": nrm(ks[4], (L, N_POOL_GROUPS, POOL_GROUP, POOL_GROUP), POOL_GROUP),
        "pool_scale": 1.0 + 0.1 * jax.random.normal(ks[5], (L, POOL_WIDTH), f32),
        "q_a_norm_gain": gain(ks[6], Q_LORA),
        "w_uq": nrm(ks[7], (L, Q_LORA, N_HEADS * QK_DIM), Q_LORA),
        "kv_a_norm_gain": gain(ks[8], KV_LORA),
        "w_ukv": nrm(ks[9], (L, KV_LORA, N_HEADS * (NOPE_DIM + V_DIM)), KV_LORA),
        "q_norm_gain": gain(ks[10], QK_DIM),
        "k_norm_gain": gain(ks[11], QK_DIM),
        "w_branch_pool": nrm(ks[12], (L, POOL_WIDTH, D_MODEL), POOL_WIDTH),
        "w_branch_mla": nrm(ks[13], (L, MLA_WIDTH, D_MODEL), MLA_WIDTH),
        "w_o": nrm(ks[14], (L, D_MODEL, D_MODEL), D_MODEL),
        "norm_ffn_gain": gain(ks[15], D_MODEL),
        "w_up": nrm(ks[16], (L, D_MODEL, 2 * D_FF), D_MODEL),
        "conv_w": nrm(ks[17], (L, CONV_W, 2 * D_FF), CONV_W),
        "conv_b": 0.01 * jax.random.normal(ks[18], (L, 2 * D_FF), f32),
        "w_down": nrm(ks[19], (L, D_FF, D_MODEL), D_FF),
    }


def reference(x_prompt, x_sample, norm_mix_gain, w_in, pool_w, pool_scale, q_a_norm_gain, w_uq,
              kv_a_norm_gain, w_ukv, q_norm_gain, k_norm_gain, w_branch_pool, w_branch_mla, w_o,
              norm_ffn_gain, w_up, conv_w, conv_b, w_down):
    y_prompt = _trunk(x_prompt, norm_mix_gain, w_in, pool_w, pool_scale, q_a_norm_gain, w_uq,
                      kv_a_norm_gain, w_ukv, q_norm_gain, k_norm_gain, w_branch_pool,
                      w_branch_mla, w_o, norm_ffn_gain, w_up, conv_w, conv_b, w_down)
    y_sample = _trunk(x_sample, norm_mix_gain, w_in, pool_w, pool_scale, q_a_norm_gain, w_uq,
                      kv_a_norm_gain, w_ukv, q_norm_gain, k_norm_gain, w_branch_pool,
                      w_branch_mla, w_o, norm_ffn_gain, w_up, conv_w, conv_b, w_down)
    return (y_prompt, y_sample)
```

```python
import functools
import math

import jax
import jax.numpy as jnp
from jax import lax
from jax.experimental import pallas as pl
from jax.experimental.pallas import tpu as pltpu

F32 = jnp.float32
BF16 = jnp.bfloat16

RMS_EPS = 1e-6
ROPE_THETA = 10000.0
N_HEADS = 16
NOPE_DIM = 128
ROPE_DIM = 64
V_DIM = 128
QK_DIM = NOPE_DIM + ROPE_DIM
HEAD_PAD = 256
POOL_WINDOWS = (2, 4, 8, 16)
POOL_HALO = 8
CONV_HALO = 16

V7X_VMEM_LIMIT = 56 * 1024 * 1024


def _params(*sem):
    return pltpu.CompilerParams(dimension_semantics=sem, vmem_limit_bytes=V7X_VMEM_LIMIT)


def _rmsnorm_kernel(x_ref, g_ref, o_ref):
    x = x_ref[...]
    ms = jnp.mean(x * x, axis=-1, keepdims=True)
    o_ref[...] = (x * lax.rsqrt(ms + RMS_EPS) * g_ref[...]).astype(o_ref.dtype)


def _rmsnorm(x, gain, tr=256):
    m, d = x.shape
    return pl.pallas_call(
        _rmsnorm_kernel,
        grid=(m // tr,),
        in_specs=[pl.BlockSpec((tr, d), lambda i: (i, 0)),
                  pl.BlockSpec((1, d), lambda i: (0, 0))],
        out_specs=pl.BlockSpec((tr, d), lambda i: (i, 0)),
        out_shape=jax.ShapeDtypeStruct((m, d), BF16),
        compiler_params=_params("parallel"),
        name="rmsnorm",
    )(x, gain.reshape(1, d))


def _sigmoid(x):
    return 1.0 / (1.0 + jnp.exp(-x))


def _mm_kernel(a_ref, b_ref, o_ref, *, gate):
    acc = jnp.dot(a_ref[...], b_ref[...], preferred_element_type=F32)
    if gate:
        acc = _sigmoid(acc)
    o_ref[...] = acc.astype(o_ref.dtype)


def _matmul(a, b, *, tm, tn, out_dtype, gate=False, name):
    m, k = a.shape
    _, n = b.shape
    return pl.pallas_call(
        functools.partial(_mm_kernel, gate=gate),
        grid=(m // tm, n // tn),
        in_specs=[pl.BlockSpec((tm, k), lambda i, j: (i, 0)),
                  pl.BlockSpec((k, tn), lambda i, j: (0, j))],
        out_specs=pl.BlockSpec((tm, tn), lambda i, j: (i, j)),
        out_shape=jax.ShapeDtypeStruct((m, n), out_dtype),
        compiler_params=_params("parallel", "arbitrary"),
        name=name,
    )(a, b)


def _mm_res_kernel(a_ref, b_ref, r_ref, o_ref):
    acc = jnp.dot(a_ref[...], b_ref[...], preferred_element_type=F32)
    o_ref[...] = r_ref[...] + acc


def _matmul_residual(a, b, res, *, tm, tn, name):
    m, k = a.shape
    _, n = b.shape
    return pl.pallas_call(
        _mm_res_kernel,
        grid=(m // tm, n // tn),
        in_specs=[pl.BlockSpec((tm, k), lambda i, j: (i, 0)),
                  pl.BlockSpec((k, tn), lambda i, j: (0, j)),
                  pl.BlockSpec((tm, tn), lambda i, j: (i, j))],
        out_specs=pl.BlockSpec((tm, tn), lambda i, j: (i, j)),
        out_shape=jax.ShapeDtypeStruct((m, n), F32),
        compiler_params=_params("parallel", "arbitrary"),
        name=name,
    )(a, b, res)


def _pool_kernel(u_ref, prev_ref, next_ref, w_ref, scale_ref, o_ref, *, tm, seq):
    i = pl.program_id(0)
    g = pl.program_id(1)
    half = jnp.left_shift(1, g)
    pos0 = (i * tm) % seq
    u = u_ref[...]
    prev = jnp.where(pos0 > 0, prev_ref[...], 0.0)
    nxt = jnp.where(pos0 + tm < seq, next_ref[...], 0.0)
    ext = jnp.concatenate([prev, u, nxt], axis=0)
    rr = lax.broadcasted_iota(jnp.int32, (tm, tm + 2 * POOL_HALO), 0)
    cc = lax.broadcasted_iota(jnp.int32, (tm, tm + 2 * POOL_HALO), 1) - POOL_HALO
    band = jnp.where((cc >= rr - half) & (cc < rr + half), 1.0, 0.0).astype(BF16)
    hi = ext.astype(BF16)
    lo = (ext - hi.astype(F32)).astype(BF16)
    wsum = (jnp.dot(band, hi, preferred_element_type=F32)
            + jnp.dot(band, lo, preferred_element_type=F32))
    t = pos0 + lax.broadcasted_iota(jnp.int32, (tm, 1), 0)
    cnt = (jnp.minimum(t + half, seq) - jnp.maximum(t - half, 0)).astype(F32)
    pooled = wsum / cnt - u
    y = jnp.dot(pooled.astype(BF16), w_ref[0], preferred_element_type=F32)
    o_ref[...] = (y * scale_ref[...]).astype(o_ref.dtype)


def _pool_mixer(zs, pool_w, pool_scale, *, seq, tm=256):
    m = zs.shape[0]
    ng, c, _ = pool_w.shape
    hb = tm // POOL_HALO
    last = m // POOL_HALO - 1
    return pl.pallas_call(
        functools.partial(_pool_kernel, tm=tm, seq=seq),
        grid=(m // tm, ng),
        in_specs=[pl.BlockSpec((tm, c), lambda i, g: (i, g)),
                  pl.BlockSpec((POOL_HALO, c), lambda i, g: (jnp.maximum(i * hb - 1, 0), g)),
                  pl.BlockSpec((POOL_HALO, c), lambda i, g: (jnp.minimum((i + 1) * hb, last), g)),
                  pl.BlockSpec((1, c, c), lambda i, g: (g, 0, 0)),
                  pl.BlockSpec((1, c), lambda i, g: (0, g))],
        out_specs=pl.BlockSpec((tm, c), lambda i, g: (i, g)),
        out_shape=jax.ShapeDtypeStruct((m, ng * c), BF16),
        compiler_params=_params("parallel", "arbitrary"),
        name="pool_mixer",
    )(zs, zs, zs, pool_w, pool_scale.reshape(1, ng * c))


def _fold_rope(t):
    folded = t + pltpu.roll(t, ROPE_DIM, axis=1)
    lane = lax.broadcasted_iota(jnp.int32, t.shape, 1)
    return jnp.where(lane < ROPE_DIM, folded, 0.0)


def _qproj_kernel(c_ref, ga_ref, w_ref, gn_ref, gr_ref, tab_ref, o_ref, a_scr, *, out_scale):
    @pl.when(pl.program_id(1) == 0)
    def _():
        x = c_ref[...]
        ms = jnp.mean(x * x, axis=-1, keepdims=True)
        a_scr[...] = (x * lax.rsqrt(ms + RMS_EPS) * ga_ref[...]).astype(BF16)

    y = jnp.dot(a_scr[...], w_ref[0], preferred_element_type=F32)
    nope = y[:, :NOPE_DIM]
    rr = y[:, NOPE_DIM:]
    ssq = jnp.sum(nope * nope, axis=-1, keepdims=True) + 0.5 * jnp.sum(rr * rr, axis=-1, keepdims=True)
    r = lax.rsqrt(ssq * (1.0 / QK_DIM) + RMS_EPS)
    o_ref[0, 0, :, :NOPE_DIM] = (nope * r * gn_ref[...] * out_scale).astype(o_ref.dtype)
    rope = _fold_rope(rr * r * gr_ref[...] * tab_ref[...])
    o_ref[0, 0, :, NOPE_DIM:] = (rope * out_scale).astype(o_ref.dtype)


def _q_proj(zs, col_block, ga, wq, gn, gr, tab, *, n_seq, seq, out_scale, tm=512):
    m = zs.shape[0]
    nh, kq, _ = wq.shape
    spt = seq // tm
    return pl.pallas_call(
        functools.partial(_qproj_kernel, out_scale=out_scale),
        grid=(m // tm, nh),
        in_specs=[pl.BlockSpec((tm, kq), lambda i, h: (i, col_block)),
                  pl.BlockSpec((1, kq), lambda i, h: (0, 0)),
                  pl.BlockSpec((1, kq, HEAD_PAD), lambda i, h: (h, 0, 0)),
                  pl.BlockSpec((1, NOPE_DIM), lambda i, h: (0, 0)),
                  pl.BlockSpec((1, 2 * ROPE_DIM), lambda i, h: (0, 0)),
                  pl.BlockSpec((tm, 2 * ROPE_DIM), lambda i, h: (i % spt, 0))],
        out_specs=pl.BlockSpec((1, 1, tm, HEAD_PAD), lambda i, h: (i // spt, h, i % spt, 0)),
        out_shape=jax.ShapeDtypeStruct((n_seq, nh, seq, HEAD_PAD), BF16),
        scratch_shapes=[pltpu.VMEM((tm, kq), BF16)],
        compiler_params=_params("parallel", "arbitrary"),
        name="q_proj",
    )(zs, ga, wq, gn, gr, tab)


def _kvproj_kernel(c_ref, kr_ref, ga_ref, w_ref, gn_ref, gr_ref, tab_ref, k_ref, v_ref,
                   a_scr, rope_scr, ssq_scr):
    @pl.when(pl.program_id(1) == 0)
    def _():
        x = c_ref[...]
        ms = jnp.mean(x * x, axis=-1, keepdims=True)
        a_scr[...] = (x * lax.rsqrt(ms + RMS_EPS) * ga_ref[...]).astype(BF16)
        kr = kr_ref[...]
        ssq_scr[...] = jnp.broadcast_to(0.5 * jnp.sum(kr * kr, axis=-1, keepdims=True), ssq_scr.shape)
        rope_scr[...] = _fold_rope(kr * gr_ref[...] * tab_ref[...])

    y = jnp.dot(a_scr[...], w_ref[0], preferred_element_type=F32)
    nope = y[:, :NOPE_DIM]
    ssq = jnp.sum(nope * nope, axis=-1, keepdims=True) + ssq_scr[:, :1]
    r = lax.rsqrt(ssq * (1.0 / QK_DIM) + RMS_EPS)
    k_ref[0, 0, :, :NOPE_DIM] = (nope * r * gn_ref[...]).astype(k_ref.dtype)
    k_ref[0, 0, :, NOPE_DIM:] = (rope_scr[...] * r).astype(k_ref.dtype)
    v_ref[0, 0] = y[:, NOPE_DIM:].astype(v_ref.dtype)


def _kv_proj(zs, ckv_block, kr_block, ga, wkv, gn, gr, tab, *, n_seq, seq, tm=512):
    m = zs.shape[0]
    nh, kc, _ = wkv.shape
    spt = seq // tm
    return pl.pallas_call(
        _kvproj_kernel,
        grid=(m // tm, nh),
        in_specs=[pl.BlockSpec((tm, kc), lambda i, h: (i, ckv_block)),
                  pl.BlockSpec((tm, 2 * ROPE_DIM), lambda i, h: (i, kr_block)),
                  pl.BlockSpec((1, kc), lambda i, h: (0, 0)),
                  pl.BlockSpec((1, kc, NOPE_DIM + V_DIM), lambda i, h: (h, 0, 0)),
                  pl.BlockSpec((1, NOPE_DIM), lambda i, h: (0, 0)),
                  pl.BlockSpec((1, 2 * ROPE_DIM), lambda i, h: (0, 0)),
                  pl.BlockSpec((tm, 2 * ROPE_DIM), lambda i, h: (i % spt, 0))],
        out_specs=[pl.BlockSpec((1, 1, tm, HEAD_PAD), lambda i, h: (i // spt, h, i % spt, 0)),
                   pl.BlockSpec((1, 1, tm, V_DIM), lambda i, h: (i // spt, h, i % spt, 0))],
        out_shape=[jax.ShapeDtypeStruct((n_seq, nh, seq, HEAD_PAD), BF16),
                   jax.ShapeDtypeStruct((n_seq, nh, seq, V_DIM), BF16)],
        scratch_shapes=[pltpu.VMEM((tm, kc), BF16),
                        pltpu.VMEM((tm, 2 * ROPE_DIM), F32),
                        pltpu.VMEM((tm, 128), F32)],
        compiler_params=_params("parallel", "arbitrary"),
        name="kv_proj",
    )(zs, zs, ga, wkv, gn, gr, tab)


def _attn_kernel(q_ref, k_ref, v_ref, o_ref, *, tk):
    q = q_ref[0, 0]
    tq = q.shape[0]
    nk = k_ref.shape[2] // tk

    def body(j, carry):
        m, l, acc = carry
        start = pl.multiple_of(j * tk, tk)
        k = k_ref[0, 0, pl.ds(start, tk), :]
        v = v_ref[0, 0, pl.ds(start, tk), :]
        s = lax.dot_general(q, k, (((1,), (1,)), ((), ())), preferred_element_type=F32)
        m_new = jnp.maximum(m, jnp.max(s, axis=-1, keepdims=True))
        alpha = jnp.exp2(m - m_new)
        p = jnp.exp2(s - m_new)
        l = alpha * l + jnp.sum(p, axis=-1, keepdims=True)
        acc = alpha * acc + jnp.dot(p.astype(BF16), v, preferred_element_type=F32)
        return m_new, l, acc

    init = (jnp.full((tq, 1), -jnp.inf, F32), jnp.zeros((tq, 1), F32), jnp.zeros((tq, V_DIM), F32))
    _, l, acc = lax.fori_loop(0, nk, body, init)
    o_ref[0] = (acc / l).astype(o_ref.dtype)


def _attention(q, k, v, *, tq=512, tk=512):
    n_seq, nh, seq, _ = q.shape
    return pl.pallas_call(
        functools.partial(_attn_kernel, tk=tk),
        grid=(n_seq, nh, seq // tq),
        in_specs=[pl.BlockSpec((1, 1, tq, HEAD_PAD), lambda b, h, i: (b, h, i, 0)),
                  pl.BlockSpec((1, 1, seq, HEAD_PAD), lambda b, h, i: (b, h, 0, 0)),
                  pl.BlockSpec((1, 1, seq, V_DIM), lambda b, h, i: (b, h, 0, 0))],
        out_specs=pl.BlockSpec((1, tq, V_DIM), lambda b, h, i: (b, i, h)),
        out_shape=jax.ShapeDtypeStruct((n_seq, seq, nh * V_DIM), BF16),
        compiler_params=_params("parallel", "parallel", "arbitrary"),
        name="attention",
    )(q, k, v)


def _merge_kernel(a_ref, b_ref, wa_ref, wb_ref, ga_ref, gb_ref, o_ref):
    pa = jnp.dot(a_ref[...], wa_ref[...], preferred_element_type=F32)
    pb = jnp.dot(b_ref[...], wb_ref[...], preferred_element_type=F32)
    o_ref[...] = (ga_ref[...].astype(F32) * pa + gb_ref[...].astype(F32) * pb).astype(o_ref.dtype)


def _gated_merge(a, b, wa, wb, gates, *, tm=1024, tn=512):
    m, k = a.shape
    n = wa.shape[1]
    nj = n // tn
    return pl.pallas_call(
        _merge_kernel,
        grid=(m // tm, nj),
        in_specs=[pl.BlockSpec((tm, k), lambda i, j: (i, 0)),
                  pl.BlockSpec((tm, k), lambda i, j: (i, 0)),
                  pl.BlockSpec((k, tn), lambda i, j: (0, j)),
                  pl.BlockSpec((k, tn), lambda i, j: (0, j)),
                  pl.BlockSpec((tm, tn), lambda i, j: (i, j)),
                  pl.BlockSpec((tm, tn), lambda i, j: (i, j + nj))],
        out_specs=pl.BlockSpec((tm, tn), lambda i, j: (i, j)),
        out_shape=jax.ShapeDtypeStruct((m, n), BF16),
        compiler_params=_params("parallel", "arbitrary"),
        name="gated_merge",
    )(a, b, wa, wb, gates, gates)


def _ffn_up_kernel(h_ref, prev_ref, next_ref, wg_ref, wv_ref, cwg_ref, cwv_ref, cbg_ref, cbv_ref,
                   o_ref, a_scr, *, tm, seq):
    @pl.when(pl.program_id(1) == 0)
    def _():
        pos0 = (pl.program_id(0) * tm) % seq
        a_scr[:CONV_HALO] = jnp.where(pos0 > 0, prev_ref[...], jnp.zeros_like(prev_ref))
        a_scr[CONV_HALO:CONV_HALO + tm] = h_ref[...]
        a_scr[CONV_HALO + tm:] = jnp.where(pos0 + tm < seq, next_ref[...], jnp.zeros_like(next_ref))

    a = a_scr[...]

    def conv(w_ref, cw_ref, cb_ref):
        up = jnp.dot(a, w_ref[...], preferred_element_type=F32)
        cw = cw_ref[...]
        return (up[CONV_HALO - 1:CONV_HALO - 1 + tm] * cw[0:1]
                + up[CONV_HALO:CONV_HALO + tm] * cw[1:2]
                + up[CONV_HALO + 1:CONV_HALO + 1 + tm] * cw[2:3]
                + cb_ref[...])

    gate = conv(wg_ref, cwg_ref, cbg_ref)
    val = conv(wv_ref, cwv_ref, cbv_ref)
    o_ref[...] = (gate * _sigmoid(gate) * val).astype(o_ref.dtype)


def _ffn_up(h, w_up, conv_w, conv_b, *, seq, tm=1024, tn=256):
    m, d = h.shape
    dff = w_up.shape[1] // 2
    nj = dff // tn
    hb = tm // CONV_HALO
    last = m // CONV_HALO - 1
    return pl.pallas_call(
        functools.partial(_ffn_up_kernel, tm=tm, seq=seq),
        grid=(m // tm, nj),
        in_specs=[pl.BlockSpec((tm, d), lambda i, j: (i, 0)),
                  pl.BlockSpec((CONV_HALO, d), lambda i, j: (jnp.maximum(i * hb - 1, 0), 0)),
                  pl.BlockSpec((CONV_HALO, d), lambda i, j: (jnp.minimum((i + 1) * hb, last), 0)),
                  pl.BlockSpec((d, tn), lambda i, j: (0, j)),
                  pl.BlockSpec((d, tn), lambda i, j: (0, j + nj)),
                  pl.BlockSpec((3, tn), lambda i, j: (0, j)),
                  pl.BlockSpec((3, tn), lambda i, j: (0, j + nj)),
                  pl.BlockSpec((1, tn), lambda i, j: (0, j)),
                  pl.BlockSpec((1, tn), lambda i, j: (0, j + nj))],
        out_specs=pl.BlockSpec((tm, tn), lambda i, j: (i, j)),
        out_shape=jax.ShapeDtypeStruct((m, dff), BF16),
        scratch_shapes=[pltpu.VMEM((tm + 2 * CONV_HALO, d), BF16)],
        compiler_params=_params("parallel", "arbitrary"),
        name="ffn_up",
    )(h, h, h, w_up, w_up, conv_w, conv_w, conv_b, conv_b)


def _swap_halves(a):
    h = a.shape[-1] // 2
    return jnp.concatenate([a[..., h:], a[..., :h]], axis=-1)


def _layer(x, n_seq, seq, norm_mix_gain, w_in, pool_w, pool_scale, q_a_norm_gain, w_uq,
           kv_a_norm_gain, w_ukv, q_norm_gain, k_norm_gain, w_branch_pool, w_branch_mla, w_o,
           norm_ffn_gain, w_up, conv_w, conv_b, w_down):
    d = x.shape[1]
    ng, pc, _ = pool_w.shape
    pool_width = ng * pc
    q_lora = q_a_norm_gain.shape[0]
    kv_lora = kv_a_norm_gain.shape[0]
    off_cq = pool_width
    off_ckv = off_cq + q_lora
    off_kr = off_ckv + kv_lora
    off_gp = off_kr + ROPE_DIM

    w_kr = w_in[:, off_kr:off_gp]
    small_cols = off_kr + 2 * ROPE_DIM
    small_pad = (-small_cols) % 768
    w_small = jnp.concatenate(
        [w_in[:, :off_kr], w_kr, _swap_halves(w_kr), jnp.zeros((d, small_pad), F32)], axis=1).astype(BF16)
    w_gate = w_in[:, off_gp:].astype(BF16)

    wq = w_uq.reshape(q_lora, N_HEADS, QK_DIM)
    wq = jnp.concatenate([wq, _swap_halves(wq[..., NOPE_DIM:])], axis=-1)
    wq = wq.transpose(1, 0, 2).astype(BF16)
    wkv = w_ukv.reshape(kv_lora, N_HEADS, NOPE_DIM + V_DIM).transpose(1, 0, 2).astype(BF16)

    def gains(g):
        gr = g[NOPE_DIM:]
        return g[:NOPE_DIM].reshape(1, NOPE_DIM), jnp.concatenate([gr, _swap_halves(gr)]).reshape(1, 2 * ROPE_DIM)

    qgn, qgr = gains(q_norm_gain)
    kgn, kgr = gains(k_norm_gain)

    inv = 1.0 / (ROPE_THETA ** (jnp.arange(0, ROPE_DIM, 2, dtype=F32) / ROPE_DIM))
    ang = jnp.arange(seq, dtype=F32)[:, None] * inv[None, :]
    cos, sin = jnp.cos(ang), jnp.sin(ang)
    rope_tab = jnp.concatenate([cos, cos, -sin, sin], axis=1)

    h = _rmsnorm(x, norm_mix_gain)
    zs = _matmul(h, w_small, tm=1024, tn=768, out_dtype=F32, name="in_proj_small")
    gates = _matmul(h, w_gate, tm=1024, tn=1024, out_dtype=BF16, gate=True, name="in_proj_gates")

    a_out = _pool_mixer(zs, pool_w.astype(BF16), pool_scale, seq=seq)

    q_scale = math.log2(math.e) / math.sqrt(QK_DIM)
    q = _q_proj(zs, off_cq // q_lora, q_a_norm_gain.reshape(1, q_lora), wq, qgn, qgr, rope_tab,
                n_seq=n_seq, seq=seq, out_scale=q_scale)
    k, v = _kv_proj(zs, off_ckv // kv_lora, off_kr // (2 * ROPE_DIM), kv_a_norm_gain.reshape(1, kv_lora),
                    wkv, kgn, kgr, rope_tab, n_seq=n_seq, seq=seq)
    b_out = _attention(q, k, v).reshape(n_seq * seq, N_HEADS * V_DIM)

    merged = _gated_merge(a_out, b_out, w_branch_pool.astype(BF16), w_branch_mla.astype(BF16), gates)
    x1 = _matmul_residual(merged, w_o.astype(BF16), x, tm=1024, tn=512, name="out_proj")

    h2 = _rmsnorm(x1, norm_ffn_gain)
    act = _ffn_up(h2, w_up.astype(BF16), conv_w, conv_b.reshape(1, -1), seq=seq)
    return _matmul_residual(act, w_down.astype(BF16), x1, tm=512, tn=256, name="down_proj")


def kernel(x_prompt, x_sample, norm_mix_gain, w_in, pool_w, pool_scale, q_a_norm_gain, w_uq,
           kv_a_norm_gain, w_ukv, q_norm_gain, k_norm_gain, w_branch_pool, w_branch_mla, w_o,
           norm_ffn_gain, w_up, conv_w, conv_b, w_down):
    bp, seq, d = x_prompt.shape
    bs = x_sample.shape[0]
    assert x_sample.shape[1:] == (seq, d)
    n_seq = bp + bs
    x = jnp.concatenate([x_prompt, x_sample], axis=0).reshape(n_seq * seq, d)
    y = x
    for l in range(norm_mix_gain.shape[0]):
        y = _layer(y, n_seq, seq, norm_mix_gain[l], w_in[l], pool_w[l], pool_scale[l], q_a_norm_gain[l],
                   w_uq[l], kv_a_norm_gain[l], w_ukv[l], q_norm_gain[l], k_norm_gain[l],
                   w_branch_pool[l], w_branch_mla[l], w_o[l], norm_ffn_gain[l], w_up[l],
                   conv_w[l], conv_b[l], w_down[l])
    y = y.reshape(n_seq, seq, d)
    return (y[:bp], y[bp:])
```

```python
import functools
import math

import jax
import jax.numpy as jnp
from jax import lax
from jax.experimental import pallas as pl
from jax.experimental.pallas import tpu as pltpu

F32 = jnp.float32
BF16 = jnp.bfloat16

RMS_EPS = 1e-6
ROPE_THETA = 10000.0
N_HEADS = 16
NOPE_DIM = 128
ROPE_DIM = 64
V_DIM = 128
QK_DIM = NOPE_DIM + ROPE_DIM
HEAD_PAD = 256
POOL_WINDOWS = (2, 4, 8, 16)
POOL_HALO = 8
CONV_HALO = 16
ATTN_KEY_TILE = 512

V7X_VMEM_LIMIT = 56 * 1024 * 1024


def _params(*sem):
    return pltpu.CompilerParams(dimension_semantics=sem, vmem_limit_bytes=V7X_VMEM_LIMIT)


def _rmsnorm_kernel(x_ref, g_ref, o_ref):
    x = x_ref[...]
    ms = jnp.mean(x * x, axis=-1, keepdims=True)
    o_ref[...] = (x * lax.rsqrt(ms + RMS_EPS) * g_ref[...]).astype(o_ref.dtype)


def _rmsnorm(x, gain, tr=256):
    m, d = x.shape
    return pl.pallas_call(
        _rmsnorm_kernel,
        grid=(m // tr,),
        in_specs=[pl.BlockSpec((tr, d), lambda i: (i, 0)),
                  pl.BlockSpec((1, d), lambda i: (0, 0))],
        out_specs=pl.BlockSpec((tr, d), lambda i: (i, 0)),
        out_shape=jax.ShapeDtypeStruct((m, d), BF16),
        compiler_params=_params("parallel"),
        name="rmsnorm",
    )(x, gain.reshape(1, d))


def _sigmoid(x):
    return 1.0 / (1.0 + jnp.exp(-x))


def _mm_kernel(a_ref, b_ref, o_ref, *, gate):
    acc = jnp.dot(a_ref[...], b_ref[...], preferred_element_type=F32)
    if gate:
        acc = _sigmoid(acc)
    o_ref[...] = acc.astype(o_ref.dtype)


def _matmul(a, b, *, tm, tn, out_dtype, gate=False, name):
    m, k = a.shape
    _, n = b.shape
    return pl.pallas_call(
        functools.partial(_mm_kernel, gate=gate),
        grid=(m // tm, n // tn),
        in_specs=[pl.BlockSpec((tm, k), lambda i, j: (i, 0)),
                  pl.BlockSpec((k, tn), lambda i, j: (0, j))],
        out_specs=pl.BlockSpec((tm, tn), lambda i, j: (i, j)),
        out_shape=jax.ShapeDtypeStruct((m, n), out_dtype),
        compiler_params=_params("parallel", "arbitrary"),
        name=name,
    )(a, b)


def _mm_res_kernel(a_ref, b_ref, r_ref, o_ref):
    acc = jnp.dot(a_ref[...], b_ref[...], preferred_element_type=F32)
    o_ref[...] = r_ref[...] + acc


def _matmul_residual(a, b, res, *, tm, tn, name):
    m, k = a.shape
    _, n = b.shape
    return pl.pallas_call(
        _mm_res_kernel,
        grid=(m // tm, n // tn),
        in_specs=[pl.BlockSpec((tm, k), lambda i, j: (i, 0)),
                  pl.BlockSpec((k, tn), lambda i, j: (0, j)),
                  pl.BlockSpec((tm, tn), lambda i, j: (i, j))],
        out_specs=pl.BlockSpec((tm, tn), lambda i, j: (i, j)),
        out_shape=jax.ShapeDtypeStruct((m, n), F32),
        compiler_params=_params("parallel", "arbitrary"),
        name=name,
    )(a, b, res)


def _pool_kernel(u_ref, prev_ref, next_ref, w_ref, scale_ref, o_ref, *, tm, seq):
    i = pl.program_id(0)
    g = pl.program_id(1)
    half = jnp.left_shift(1, g)
    pos0 = (i * tm) % seq
    u = u_ref[...]
    prev = jnp.where(pos0 > 0, prev_ref[...], 0.0)
    nxt = jnp.where(pos0 + tm < seq, next_ref[...], 0.0)
    ext = jnp.concatenate([prev, u, nxt], axis=0)
    rr = lax.broadcasted_iota(jnp.int32, (tm, tm + 2 * POOL_HALO), 0)
    cc = lax.broadcasted_iota(jnp.int32, (tm, tm + 2 * POOL_HALO), 1) - POOL_HALO
    band = jnp.where((cc >= rr - half) & (cc < rr + half), 1.0, 0.0).astype(BF16)
    hi = ext.astype(BF16)
    lo = (ext - hi.astype(F32)).astype(BF16)
    wsum = (jnp.dot(band, hi, preferred_element_type=F32)
            + jnp.dot(band, lo, preferred_element_type=F32))
    t = pos0 + lax.broadcasted_iota(jnp.int32, (tm, 1), 0)
    cnt = (jnp.minimum(t + half, seq) - jnp.maximum(t - half, 0)).astype(F32)
    pooled = wsum / cnt - u
    y = jnp.dot(pooled.astype(BF16), w_ref[0], preferred_element_type=F32)
    o_ref[...] = (y * scale_ref[...]).astype(o_ref.dtype)


def _pool_mixer(zs, pool_w, pool_scale, *, seq, tm=256):
    m = zs.shape[0]
    ng, c, _ = pool_w.shape
    hb = tm // POOL_HALO
    last = m // POOL_HALO - 1
    return pl.pallas_call(
        functools.partial(_pool_kernel, tm=tm, seq=seq),
        grid=(m // tm, ng),
        in_specs=[pl.BlockSpec((tm, c), lambda i, g: (i, g)),
                  pl.BlockSpec((POOL_HALO, c), lambda i, g: (jnp.maximum(i * hb - 1, 0), g)),
                  pl.BlockSpec((POOL_HALO, c), lambda i, g: (jnp.minimum((i + 1) * hb, last), g)),
                  pl.BlockSpec((1, c, c), lambda i, g: (g, 0, 0)),
                  pl.BlockSpec((1, c), lambda i, g: (0, g))],
        out_specs=pl.BlockSpec((tm, c), lambda i, g: (i, g)),
        out_shape=jax.ShapeDtypeStruct((m, ng * c), BF16),
        compiler_params=_params("parallel", "arbitrary"),
        name="pool_mixer",
    )(zs, zs, zs, pool_w, pool_scale.reshape(1, ng * c))


def _fold_rope(t):
    folded = t + pltpu.roll(t, ROPE_DIM, axis=1)
    lane = lax.broadcasted_iota(jnp.int32, t.shape, 1)
    return jnp.where(lane < ROPE_DIM, folded, 0.0)


def _qproj_kernel(c_ref, ga_ref, w_ref, gn_ref, gr_ref, tab_ref, o_ref, *, out_scale):
    x = c_ref[...]
    ms = jnp.mean(x * x, axis=-1, keepdims=True)
    a = (x * lax.rsqrt(ms + RMS_EPS) * ga_ref[...]).astype(BF16)
    gn = gn_ref[...] * out_scale
    rope_mul = gr_ref[...] * tab_ref[...] * out_scale
    for h in range(w_ref.shape[0]):
        y = jnp.dot(a, w_ref[h], preferred_element_type=F32)
        nope = y[:, :NOPE_DIM]
        rr = y[:, NOPE_DIM:]
        ssq = jnp.sum(nope * nope, axis=-1, keepdims=True) + 0.5 * jnp.sum(rr * rr, axis=-1, keepdims=True)
        r = lax.rsqrt(ssq * (1.0 / QK_DIM) + RMS_EPS)
        o_ref[0, h, :, :NOPE_DIM] = (nope * r * gn).astype(o_ref.dtype)
        o_ref[0, h, :, NOPE_DIM:] = _fold_rope(rr * r * rope_mul).astype(o_ref.dtype)


def _q_proj(zs, col_block, ga, wq, gn, gr, tab, *, n_seq, seq, out_scale, tm=512):
    m = zs.shape[0]
    nh, kq, _ = wq.shape
    spt = seq // tm
    return pl.pallas_call(
        functools.partial(_qproj_kernel, out_scale=out_scale),
        grid=(m // tm,),
        in_specs=[pl.BlockSpec((tm, kq), lambda i: (i, col_block)),
                  pl.BlockSpec((1, kq), lambda i: (0, 0)),
                  pl.BlockSpec((nh, kq, HEAD_PAD), lambda i: (0, 0, 0)),
                  pl.BlockSpec((1, NOPE_DIM), lambda i: (0, 0)),
                  pl.BlockSpec((1, 2 * ROPE_DIM), lambda i: (0, 0)),
                  pl.BlockSpec((tm, 2 * ROPE_DIM), lambda i: (i % spt, 0))],
        out_specs=pl.BlockSpec((1, nh, tm, HEAD_PAD), lambda i: (i // spt, 0, i % spt, 0)),
        out_shape=jax.ShapeDtypeStruct((n_seq, nh, seq, HEAD_PAD), BF16),
        compiler_params=_params("parallel"),
        name="q_proj",
    )(zs, ga, wq, gn, gr, tab)


def _kvproj_kernel(c_ref, kr_ref, ga_ref, wk_ref, wvt_ref, gn_ref, gr_ref, tab_ref, k_ref, vt_ref):
    x = c_ref[...]
    ms = jnp.mean(x * x, axis=-1, keepdims=True)
    a = (x * lax.rsqrt(ms + RMS_EPS) * ga_ref[...]).astype(BF16)
    kr = kr_ref[...]
    ssq_rope = 0.5 * jnp.sum(kr * kr, axis=-1, keepdims=True)
    rope = _fold_rope(kr * gr_ref[...] * tab_ref[...])
    gn = gn_ref[...]
    for h in range(wk_ref.shape[0]):
        nope = jnp.dot(a, wk_ref[h], preferred_element_type=F32)
        ssq = jnp.sum(nope * nope, axis=-1, keepdims=True) + ssq_rope
        r = lax.rsqrt(ssq * (1.0 / QK_DIM) + RMS_EPS)
        k_ref[0, h, :, :NOPE_DIM] = (nope * r * gn).astype(k_ref.dtype)
        k_ref[0, h, :, NOPE_DIM:] = (rope * r).astype(k_ref.dtype)
        vt = lax.dot_general(wvt_ref[h], a, (((1,), (1,)), ((), ())), preferred_element_type=F32)
        vt_ref[0, h, 0] = vt.astype(vt_ref.dtype)


def _kv_proj(zs, ckv_block, kr_block, ga, wk, wvt, gn, gr, tab, *, n_seq, seq, tm):
    m = zs.shape[0]
    nh, kc, _ = wk.shape
    spt = seq // tm
    return pl.pallas_call(
        _kvproj_kernel,
        grid=(m // tm,),
        in_specs=[pl.BlockSpec((tm, kc), lambda i: (i, ckv_block)),
                  pl.BlockSpec((tm, 2 * ROPE_DIM), lambda i: (i, kr_block)),
                  pl.BlockSpec((1, kc), lambda i: (0, 0)),
                  pl.BlockSpec((nh, kc, NOPE_DIM), lambda i: (0, 0, 0)),
                  pl.BlockSpec((nh, V_DIM, kc), lambda i: (0, 0, 0)),
                  pl.BlockSpec((1, NOPE_DIM), lambda i: (0, 0)),
                  pl.BlockSpec((1, 2 * ROPE_DIM), lambda i: (0, 0)),
                  pl.BlockSpec((tm, 2 * ROPE_DIM), lambda i: (i % spt, 0))],
        out_specs=[pl.BlockSpec((1, nh, tm, HEAD_PAD), lambda i: (i // spt, 0, i % spt, 0)),
                   pl.BlockSpec((1, nh, 1, V_DIM, tm), lambda i: (i // spt, 0, i % spt, 0, 0))],
        out_shape=[jax.ShapeDtypeStruct((n_seq, nh, seq, HEAD_PAD), BF16),
                   jax.ShapeDtypeStruct((n_seq, nh, spt, V_DIM, tm), BF16)],
        compiler_params=_params("parallel"),
        name="kv_proj",
    )(zs, zs, ga, wk, wvt, gn, gr, tab)


def _attn_kernel(q_ref, k_ref, vt_ref, o_ref, s_even, s_odd, acc_scr):
    q = q_ref[0, 0]
    tq = q.shape[0]
    nk, _, tk = vt_ref.shape[2:]
    assert nk % 2 == 0 and nk >= 2

    def scores(j, dst):
        start = pl.multiple_of(j * tk, tk)
        k = k_ref[0, 0, pl.ds(start, tk), :]
        dst[...] = lax.dot_general(k, q, (((1,), (1,)), ((), ())), preferred_element_type=F32)

    def update(j, src, m, l):
        st = src[...]
        m_new = jnp.maximum(m, jnp.max(st, axis=0, keepdims=True))
        alpha = jnp.exp2(m - m_new)
        p = jnp.exp2(st - m_new)
        l = alpha * l + jnp.sum(p, axis=0, keepdims=True)
        pv = jnp.dot(vt_ref[0, 0, j], p.astype(BF16), preferred_element_type=F32)
        acc_scr[...] = alpha * acc_scr[...] + pv
        return m_new, l

    acc_scr[...] = jnp.zeros_like(acc_scr)
    scores(0, s_even)

    def body(jj, carry):
        m, l = carry
        j = 2 * jj
        scores(j + 1, s_odd)
        m, l = update(j, s_even, m, l)
        scores(j + 2, s_even)
        m, l = update(j + 1, s_odd, m, l)
        return m, l

    init = (jnp.full((1, tq), -jnp.inf, F32), jnp.zeros((1, tq), F32))
    m, l = lax.fori_loop(0, nk // 2 - 1, body, init)
    scores(nk - 1, s_odd)
    m, l = update(nk - 2, s_even, m, l)
    m, l = update(nk - 1, s_odd, m, l)
    o_ref[0] = (acc_scr[...] / l).T.astype(o_ref.dtype)


def _attention(q, k, vt, *, tq=512):
    n_seq, nh, seq, _ = q.shape
    nk, _, tk = vt.shape[2:]
    return pl.pallas_call(
        _attn_kernel,
        grid=(n_seq, nh, seq // tq),
        in_specs=[pl.BlockSpec((1, 1, tq, HEAD_PAD), lambda b, h, i: (b, h, i, 0)),
                  pl.BlockSpec((1, 1, seq, HEAD_PAD), lambda b, h, i: (b, h, 0, 0)),
                  pl.BlockSpec((1, 1, nk, V_DIM, tk), lambda b, h, i: (b, h, 0, 0, 0))],
        out_specs=pl.BlockSpec((1, tq, V_DIM), lambda b, h, i: (b, i, h)),
        out_shape=jax.ShapeDtypeStruct((n_seq, seq, nh * V_DIM), BF16),
        scratch_shapes=[pltpu.VMEM((tk, tq), F32), pltpu.VMEM((tk, tq), F32), pltpu.VMEM((V_DIM, tq), F32)],
        compiler_params=_params("parallel", "parallel", "arbitrary"),
        name="attention",
    )(q, k, vt)


def _merge_kernel(a_ref, b_ref, wa_ref, wb_ref, ga_ref, gb_ref, o_ref):
    pa = jnp.dot(a_ref[...], wa_ref[...], preferred_element_type=F32)
    pb = jnp.dot(b_ref[...], wb_ref[...], preferred_element_type=F32)
    o_ref[...] = (ga_ref[...].astype(F32) * pa + gb_ref[...].astype(F32) * pb).astype(o_ref.dtype)


def _gated_merge(a, b, wa, wb, gates, *, tm=1024, tn=512):
    m, k = a.shape
    n = wa.shape[1]
    nj = n // tn
    return pl.pallas_call(
        _merge_kernel,
        grid=(m // tm, nj),
        in_specs=[pl.BlockSpec((tm, k), lambda i, j: (i, 0)),
                  pl.BlockSpec((tm, k), lambda i, j: (i, 0)),
                  pl.BlockSpec((k, tn), lambda i, j: (0, j)),
                  pl.BlockSpec((k, tn), lambda i, j: (0, j)),
                  pl.BlockSpec((tm, tn), lambda i, j: (i, j)),
                  pl.BlockSpec((tm, tn), lambda i, j: (i, j + nj))],
        out_specs=pl.BlockSpec((tm, tn), lambda i, j: (i, j)),
        out_shape=jax.ShapeDtypeStruct((m, n), BF16),
        compiler_params=_params("parallel", "arbitrary"),
        name="gated_merge",
    )(a, b, wa, wb, gates, gates)


def _ffn_up_kernel(h_ref, prev_ref, next_ref, wg_ref, wv_ref, cwg_ref, cwv_ref, cbg_ref, cbv_ref,
                   o_ref, a_scr, ug0, uv0, ug1, uv1, *, tm, seq, nj):
    i = pl.program_id(0)
    j = pl.program_id(1)

    @pl.when(j == 0)
    def _():
        pos0 = (i * tm) % seq
        a_scr[:CONV_HALO] = jnp.where(pos0 > 0, prev_ref[...], jnp.zeros_like(prev_ref))
        a_scr[CONV_HALO:CONV_HALO + tm] = h_ref[...]
        a_scr[CONV_HALO + tm:] = jnp.where(pos0 + tm < seq, next_ref[...], jnp.zeros_like(next_ref))

    @pl.when((i == 0) & (j == 0))
    def _():
        ug1[...] = jnp.zeros_like(ug1)
        uv1[...] = jnp.zeros_like(uv1)

    def project(dst_g, dst_v):
        a = a_scr[...]
        dst_g[...] = jnp.dot(a, wg_ref[...], preferred_element_type=F32)
        dst_v[...] = jnp.dot(a, wv_ref[...], preferred_element_type=F32)

    def conv(src, cw_ref, cb_ref):
        cw = cw_ref[...]
        return (src[pl.ds(CONV_HALO - 1, tm), :] * cw[0:1]
                + src[pl.ds(CONV_HALO, tm), :] * cw[1:2]
                + src[pl.ds(CONV_HALO + 1, tm), :] * cw[2:3]
                + cb_ref[...])

    def epilogue(src_g, src_v):
        gate = conv(src_g, cwg_ref, cbg_ref)
        val = conv(src_v, cwv_ref, cbv_ref)
        o_ref[...] = (gate * _sigmoid(gate) * val).astype(o_ref.dtype)

    @pl.when((j < nj) & (j % 2 == 0))
    def _():
        epilogue(ug1, uv1)
        project(ug0, uv0)

    @pl.when((j < nj) & (j % 2 == 1))
    def _():
        epilogue(ug0, uv0)
        project(ug1, uv1)

    @pl.when(j == nj)
    def _():
        if nj % 2 == 1:
            epilogue(ug0, uv0)
        else:
            epilogue(ug1, uv1)


def _ffn_up(h, w_up, conv_w, conv_b, *, seq, tm=1024, tn=256):
    m, d = h.shape
    dff = w_up.shape[1] // 2
    nj = dff // tn
    hb = tm // CONV_HALO
    last = m // CONV_HALO - 1
    wcol = lambda j: jnp.minimum(j, nj - 1)
    ecol = lambda j: jnp.maximum(j - 1, 0)
    up_scr = pltpu.VMEM((tm + 2 * CONV_HALO, tn), F32)
    return pl.pallas_call(
        functools.partial(_ffn_up_kernel, tm=tm, seq=seq, nj=nj),
        grid=(m // tm, nj + 1),
        in_specs=[pl.BlockSpec((tm, d), lambda i, j: (i, 0)),
                  pl.BlockSpec((CONV_HALO, d), lambda i, j: (jnp.maximum(i * hb - 1, 0), 0)),
                  pl.BlockSpec((CONV_HALO, d), lambda i, j: (jnp.minimum((i + 1) * hb, last), 0)),
                  pl.BlockSpec((d, tn), lambda i, j: (0, wcol(j))),
                  pl.BlockSpec((d, tn), lambda i, j: (0, wcol(j) + nj)),
                  pl.BlockSpec((3, tn), lambda i, j: (0, ecol(j))),
                  pl.BlockSpec((3, tn), lambda i, j: (0, ecol(j) + nj)),
                  pl.BlockSpec((1, tn), lambda i, j: (0, ecol(j))),
                  pl.BlockSpec((1, tn), lambda i, j: (0, ecol(j) + nj))],
        out_specs=pl.BlockSpec((tm, tn), lambda i, j: (i, ecol(j))),
        out_shape=jax.ShapeDtypeStruct((m, dff), BF16),
        scratch_shapes=[pltpu.VMEM((tm + 2 * CONV_HALO, d), BF16), up_scr, up_scr, up_scr, up_scr],
        compiler_params=_params("arbitrary", "arbitrary"),
        name="ffn_up",
    )(h, h, h, w_up, w_up, conv_w, conv_w, conv_b, conv_b)


def _swap_halves(a):
    h = a.shape[-1] // 2
    return jnp.concatenate([a[..., h:], a[..., :h]], axis=-1)


def _layer(x, n_seq, seq, norm_mix_gain, w_in, pool_w, pool_scale, q_a_norm_gain, w_uq,
           kv_a_norm_gain, w_ukv, q_norm_gain, k_norm_gain, w_branch_pool, w_branch_mla, w_o,
           norm_ffn_gain, w_up, conv_w, conv_b, w_down):
    d = x.shape[1]
    ng, pc, _ = pool_w.shape
    pool_width = ng * pc
    q_lora = q_a_norm_gain.shape[0]
    kv_lora = kv_a_norm_gain.shape[0]
    off_cq = pool_width
    off_ckv = off_cq + q_lora
    off_kr = off_ckv + kv_lora
    off_gp = off_kr + ROPE_DIM

    w_kr = w_in[:, off_kr:off_gp]
    small_cols = off_kr + 2 * ROPE_DIM
    small_pad = (-small_cols) % 768
    w_small = jnp.concatenate(
        [w_in[:, :off_kr], w_kr, _swap_halves(w_kr), jnp.zeros((d, small_pad), F32)], axis=1).astype(BF16)
    w_gate = w_in[:, off_gp:].astype(BF16)

    wq = w_uq.reshape(q_lora, N_HEADS, QK_DIM)
    wq = jnp.concatenate([wq, _swap_halves(wq[..., NOPE_DIM:])], axis=-1)
    wq = wq.transpose(1, 0, 2).astype(BF16)
    wkv = w_ukv.reshape(kv_lora, N_HEADS, NOPE_DIM + V_DIM).astype(BF16)
    wk = wkv[..., :NOPE_DIM].transpose(1, 0, 2)
    wvt = wkv[..., NOPE_DIM:].transpose(1, 2, 0)

    def gains(g):
        gr = g[NOPE_DIM:]
        return g[:NOPE_DIM].reshape(1, NOPE_DIM), jnp.concatenate([gr, _swap_halves(gr)]).reshape(1, 2 * ROPE_DIM)

    qgn, qgr = gains(q_norm_gain)
    kgn, kgr = gains(k_norm_gain)

    inv = 1.0 / (ROPE_THETA ** (jnp.arange(0, ROPE_DIM, 2, dtype=F32) / ROPE_DIM))
    ang = jnp.arange(seq, dtype=F32)[:, None] * inv[None, :]
    cos, sin = jnp.cos(ang), jnp.sin(ang)
    rope_tab = jnp.concatenate([cos, cos, -sin, sin], axis=1)

    h = _rmsnorm(x, norm_mix_gain)
    zs = _matmul(h, w_small, tm=1024, tn=768, out_dtype=F32, name="in_proj_small")
    gates = _matmul(h, w_gate, tm=1024, tn=1024, out_dtype=BF16, gate=True, name="in_proj_gates")

    a_out = _pool_mixer(zs, pool_w.astype(BF16), pool_scale, seq=seq)

    q_scale = math.log2(math.e) / math.sqrt(QK_DIM)
    q = _q_proj(zs, off_cq // q_lora, q_a_norm_gain.reshape(1, q_lora), wq, qgn, qgr, rope_tab,
                n_seq=n_seq, seq=seq, out_scale=q_scale)
    k, vt = _kv_proj(zs, off_ckv // kv_lora, off_kr // (2 * ROPE_DIM), kv_a_norm_gain.reshape(1, kv_lora),
                     wk, wvt, kgn, kgr, rope_tab, n_seq=n_seq, seq=seq, tm=ATTN_KEY_TILE)
    b_out = _attention(q, k, vt).reshape(n_seq * seq, N_HEADS * V_DIM)

    merged = _gated_merge(a_out, b_out, w_branch_pool.astype(BF16), w_branch_mla.astype(BF16), gates)
    x1 = _matmul_residual(merged, w_o.astype(BF16), x, tm=1024, tn=512, name="out_proj")

    h2 = _rmsnorm(x1, norm_ffn_gain)
    act = _ffn_up(h2, w_up.astype(BF16), conv_w, conv_b.reshape(1, -1), seq=seq)
    return _matmul_residual(act, w_down.astype(BF16), x1, tm=512, tn=512, name="down_proj")


def kernel(x_prompt, x_sample, norm_mix_gain, w_in, pool_w, pool_scale, q_a_norm_gain, w_uq,
           kv_a_norm_gain, w_ukv, q_norm_gain, k_norm_gain, w_branch_pool, w_branch_mla, w_o,
           norm_ffn_gain, w_up, conv_w, conv_b, w_down):
    bp, seq, d = x_prompt.shape
    bs = x_sample.shape[0]
    assert x_sample.shape[1:] == (seq, d)
    n_seq = bp + bs
    x = jnp.concatenate([x_prompt, x_sample], axis=0).reshape(n_seq * seq, d)
    y = x
    for l in range(norm_mix_gain.shape[0]):
        y = _layer(y, n_seq, seq, norm_mix_gain[l], w_in[l], pool_w[l], pool_scale[l], q_a_norm_gain[l],
                   w_uq[l], kv_a_norm_gain[l], w_ukv[l], q_norm_gain[l], k_norm_gain[l],
                   w_branch_pool[l], w_branch_mla[l], w_o[l], norm_ffn_gain[l], w_up[l],
                   conv_w[l], conv_b[l], w_down[l])
    y = y.reshape(n_seq, seq, d)
    return (y[:bp], y[bp:])
```

```python
import functools
import math

import jax
import jax.numpy as jnp
from jax import lax
from jax.experimental import pallas as pl
from jax.experimental.pallas import tpu as pltpu

F32 = jnp.float32
BF16 = jnp.bfloat16

RMS_EPS = 1e-6
ROPE_THETA = 10000.0
N_HEADS = 16
NOPE_DIM = 128
ROPE_DIM = 64
V_DIM = 128
QK_DIM = NOPE_DIM + ROPE_DIM
HEAD_PAD = 256
POOL_WINDOWS = (2, 4, 8, 16)
POOL_HALO = 8
CONV_HALO = 16
FFN_ROW_CHUNKS = 3
ATTN_UNROLL = 16
ATTN_KEY_TILE = 512

V7X_VMEM_LIMIT = 56 * 1024 * 1024


def _params(*sem):
    return pltpu.CompilerParams(dimension_semantics=sem, vmem_limit_bytes=V7X_VMEM_LIMIT)


def _stack_tiles(parts, tm):
    counts = [p.shape[0] // tm for p in parts]
    offsets = [sum(counts[:k]) for k in range(len(parts))]
    return counts, offsets


def _stack_spec(block, count, offset, col=None, ncols=1):
    def index(i, *j):
        row = jnp.clip(i - offset, 0, count - 1)
        if col is None:
            return row, 0
        active = (i >= offset) & (i < offset + count)
        frozen = jnp.where(i < offset, 0, ncols - 1)
        return row, jnp.where(active, col(*j), frozen)
    return pl.BlockSpec(block, index)


def _stack_read(refs, counts, offsets, i):
    x = refs[-1][...]
    for k in range(len(refs) - 2, -1, -1):
        x = jnp.where(i < offsets[k] + counts[k], refs[k][...], x)
    return x


def _rmsnorm_kernel(*refs, counts, offsets):
    *x_refs, g_ref, o_ref = refs
    x = _stack_read(x_refs, counts, offsets, pl.program_id(0))
    ms = jnp.mean(x * x, axis=-1, keepdims=True)
    o_ref[...] = (x * lax.rsqrt(ms + RMS_EPS) * g_ref[...]).astype(o_ref.dtype)


def _rmsnorm(parts, gain, tr=256):
    d = parts[0].shape[1]
    counts, offsets = _stack_tiles(parts, tr)
    return pl.pallas_call(
        functools.partial(_rmsnorm_kernel, counts=counts, offsets=offsets),
        grid=(sum(counts),),
        in_specs=[_stack_spec((tr, d), c, o) for c, o in zip(counts, offsets)]
                 + [pl.BlockSpec((1, d), lambda i: (0, 0))],
        out_specs=pl.BlockSpec((tr, d), lambda i: (i, 0)),
        out_shape=jax.ShapeDtypeStruct((sum(counts) * tr, d), BF16),
        compiler_params=_params("arbitrary"),
        name="rmsnorm",
    )(*parts, gain.reshape(1, d))


def _sigmoid(x):
    return 1.0 / (1.0 + jnp.exp(-x))


def _mm_kernel(a_ref, b_ref, o_ref, *, gate):
    acc = jnp.dot(a_ref[...], b_ref[...], preferred_element_type=F32)
    if gate:
        acc = _sigmoid(acc)
    o_ref[...] = acc.astype(o_ref.dtype)


def _matmul(a, b, *, tm, tn, out_dtype, gate=False, name):
    m, k = a.shape
    _, n = b.shape
    return pl.pallas_call(
        functools.partial(_mm_kernel, gate=gate),
        grid=(m // tm, n // tn),
        in_specs=[pl.BlockSpec((tm, k), lambda i, j: (i, 0)),
                  pl.BlockSpec((k, tn), lambda i, j: (0, j))],
        out_specs=pl.BlockSpec((tm, tn), lambda i, j: (i, j)),
        out_shape=jax.ShapeDtypeStruct((m, n), out_dtype),
        compiler_params=_params("parallel", "arbitrary"),
        name=name,
    )(a, b)


def _mm_res_kernel(*refs, n_res, res_tiles, out_tiles):
    a_ref, b_ref = refs[:2]
    r_refs, o_refs = refs[2:2 + n_res], refs[2 + n_res:]
    i = pl.program_id(0)
    acc = jnp.dot(a_ref[...], b_ref[...], preferred_element_type=F32)
    y = _stack_read(r_refs, *res_tiles, i) + acc
    if len(o_refs) == 1:
        o_refs[0][...] = y
    else:
        for o_ref, count, offset in zip(o_refs, *out_tiles):
            @pl.when((i >= offset) & (i < offset + count))
            def _(o_ref=o_ref):
                o_ref[...] = y


def _matmul_residual(a, b, res_parts, *, out_rows=None, tm, tn, name):
    m, k = a.shape
    _, n = b.shape
    nj = n // tn
    res_tiles = _stack_tiles(res_parts, tm)
    out_rows = out_rows or (m,)
    out_tiles = [r // tm for r in out_rows], [sum(out_rows[:q]) // tm for q in range(len(out_rows))]
    col = lambda j: j
    if len(out_rows) == 1:
        out_specs = pl.BlockSpec((tm, tn), lambda i, j: (i, j))
        out_shape = jax.ShapeDtypeStruct((m, n), F32)
    else:
        out_specs = [_stack_spec((tm, tn), c, o, col, nj) for c, o in zip(*out_tiles)]
        out_shape = [jax.ShapeDtypeStruct((r, n), F32) for r in out_rows]
    return pl.pallas_call(
        functools.partial(_mm_res_kernel, n_res=len(res_parts), res_tiles=res_tiles, out_tiles=out_tiles),
        grid=(m // tm, nj),
        in_specs=[pl.BlockSpec((tm, k), lambda i, j: (i, 0)),
                  pl.BlockSpec((k, tn), lambda i, j: (0, j))]
                 + [_stack_spec((tm, tn), c, o, col, nj) for c, o in zip(*res_tiles)],
        out_specs=out_specs,
        out_shape=out_shape,
        compiler_params=_params("arbitrary", "arbitrary"),
        name=name,
    )(a, b, *res_parts)


def _pool_kernel(u_ref, prev_ref, next_ref, w_ref, scale_ref, o_ref, *, tm, seq):
    i = pl.program_id(0)
    g = pl.program_id(1)
    half = jnp.left_shift(1, g)
    pos0 = (i * tm) % seq
    u = u_ref[...]
    prev = jnp.where(pos0 > 0, prev_ref[...], 0.0)
    nxt = jnp.where(pos0 + tm < seq, next_ref[...], 0.0)
    ext = jnp.concatenate([prev, u, nxt], axis=0)
    rr = lax.broadcasted_iota(jnp.int32, (tm, tm + 2 * POOL_HALO), 0)
    cc = lax.broadcasted_iota(jnp.int32, (tm, tm + 2 * POOL_HALO), 1) - POOL_HALO
    band = jnp.where((cc >= rr - half) & (cc < rr + half), 1.0, 0.0).astype(BF16)
    hi = ext.astype(BF16)
    lo = (ext - hi.astype(F32)).astype(BF16)
    wsum = (jnp.dot(band, hi, preferred_element_type=F32)
            + jnp.dot(band, lo, preferred_element_type=F32))
    t = pos0 + lax.broadcasted_iota(jnp.int32, (tm, 1), 0)
    cnt = (jnp.minimum(t + half, seq) - jnp.maximum(t - half, 0)).astype(F32)
    pooled = wsum / cnt - u
    y = jnp.dot(pooled.astype(BF16), w_ref[0], preferred_element_type=F32)
    o_ref[...] = (y * scale_ref[...]).astype(o_ref.dtype)


def _pool_mixer(zs, pool_w, pool_scale, *, seq, tm=256):
    m = zs.shape[0]
    ng, c, _ = pool_w.shape
    hb = tm // POOL_HALO
    last = m // POOL_HALO - 1
    return pl.pallas_call(
        functools.partial(_pool_kernel, tm=tm, seq=seq),
        grid=(m // tm, ng),
        in_specs=[pl.BlockSpec((tm, c), lambda i, g: (i, g)),
                  pl.BlockSpec((POOL_HALO, c), lambda i, g: (jnp.maximum(i * hb - 1, 0), g)),
                  pl.BlockSpec((POOL_HALO, c), lambda i, g: (jnp.minimum((i + 1) * hb, last), g)),
                  pl.BlockSpec((1, c, c), lambda i, g: (g, 0, 0)),
                  pl.BlockSpec((1, c), lambda i, g: (0, g))],
        out_specs=pl.BlockSpec((tm, c), lambda i, g: (i, g)),
        out_shape=jax.ShapeDtypeStruct((m, ng * c), BF16),
        compiler_params=_params("parallel", "arbitrary"),
        name="pool_mixer",
    )(zs, zs, zs, pool_w, pool_scale.reshape(1, ng * c))


def _fold_rope(t):
    folded = t + pltpu.roll(t, ROPE_DIM, axis=1)
    lane = lax.broadcasted_iota(jnp.int32, t.shape, 1)
    return jnp.where(lane < ROPE_DIM, folded, 0.0)


def _qproj_kernel(c_ref, ga_ref, w_ref, gn_ref, gr_ref, tab_ref, o_ref, *, out_scale):
    x = c_ref[...]
    ms = jnp.mean(x * x, axis=-1, keepdims=True)
    a = (x * lax.rsqrt(ms + RMS_EPS) * ga_ref[...]).astype(BF16)
    gn = gn_ref[...] * out_scale
    rope_mul = gr_ref[...] * tab_ref[...] * out_scale
    for h in range(w_ref.shape[0]):
        y = jnp.dot(a, w_ref[h], preferred_element_type=F32)
        nope = y[:, :NOPE_DIM]
        rr = y[:, NOPE_DIM:]
        ssq = jnp.sum(nope * nope, axis=-1, keepdims=True) + 0.5 * jnp.sum(rr * rr, axis=-1, keepdims=True)
        r = lax.rsqrt(ssq * (1.0 / QK_DIM) + RMS_EPS)
        o_ref[0, h, :, :NOPE_DIM] = (nope * r * gn).astype(o_ref.dtype)
        o_ref[0, h, :, NOPE_DIM:] = _fold_rope(rr * r * rope_mul).astype(o_ref.dtype)


def _q_proj(zs, col_block, ga, wq, gn, gr, tab, *, n_seq, seq, out_scale, tm=512):
    m = zs.shape[0]
    nh, kq, _ = wq.shape
    spt = seq // tm
    return pl.pallas_call(
        functools.partial(_qproj_kernel, out_scale=out_scale),
        grid=(m // tm,),
        in_specs=[pl.BlockSpec((tm, kq), lambda i: (i, col_block)),
                  pl.BlockSpec((1, kq), lambda i: (0, 0)),
                  pl.BlockSpec((nh, kq, HEAD_PAD), lambda i: (0, 0, 0)),
                  pl.BlockSpec((1, NOPE_DIM), lambda i: (0, 0)),
                  pl.BlockSpec((1, 2 * ROPE_DIM), lambda i: (0, 0)),
                  pl.BlockSpec((tm, 2 * ROPE_DIM), lambda i: (i % spt, 0))],
        out_specs=pl.BlockSpec((1, nh, tm, HEAD_PAD), lambda i: (i // spt, 0, i % spt, 0)),
        out_shape=jax.ShapeDtypeStruct((n_seq, nh, seq, HEAD_PAD), BF16),
        compiler_params=_params("parallel"),
        name="q_proj",
    )(zs, ga, wq, gn, gr, tab)


def _kvproj_kernel(c_ref, kr_ref, ga_ref, wk_ref, wvt_ref, gn_ref, gr_ref, tab_ref, k_ref, vt_ref):
    x = c_ref[...]
    ms = jnp.mean(x * x, axis=-1, keepdims=True)
    a = (x * lax.rsqrt(ms + RMS_EPS) * ga_ref[...]).astype(BF16)
    kr = kr_ref[...]
    ssq_rope = 0.5 * jnp.sum(kr * kr, axis=-1, keepdims=True)
    rope = _fold_rope(kr * gr_ref[...] * tab_ref[...])
    gn = gn_ref[...]
    for h in range(wk_ref.shape[0]):
        nope = jnp.dot(a, wk_ref[h], preferred_element_type=F32)
        ssq = jnp.sum(nope * nope, axis=-1, keepdims=True) + ssq_rope
        r = lax.rsqrt(ssq * (1.0 / QK_DIM) + RMS_EPS)
        k_ref[0, h, :, :NOPE_DIM] = (nope * r * gn).astype(k_ref.dtype)
        k_ref[0, h, :, NOPE_DIM:] = (rope * r).astype(k_ref.dtype)
        vt = lax.dot_general(wvt_ref[h], a, (((1,), (1,)), ((), ())), preferred_element_type=F32)
        vt_ref[0, h, 0] = vt.astype(vt_ref.dtype)


def _kv_proj(zs, ckv_block, kr_block, ga, wk, wvt, gn, gr, tab, *, n_seq, seq, tm):
    m = zs.shape[0]
    nh, kc, _ = wk.shape
    spt = seq // tm
    return pl.pallas_call(
        _kvproj_kernel,
        grid=(m // tm,),
        in_specs=[pl.BlockSpec((tm, kc), lambda i: (i, ckv_block)),
                  pl.BlockSpec((tm, 2 * ROPE_DIM), lambda i: (i, kr_block)),
                  pl.BlockSpec((1, kc), lambda i: (0, 0)),
                  pl.BlockSpec((nh, kc, NOPE_DIM), lambda i: (0, 0, 0)),
                  pl.BlockSpec((nh, V_DIM, kc), lambda i: (0, 0, 0)),
                  pl.BlockSpec((1, NOPE_DIM), lambda i: (0, 0)),
                  pl.BlockSpec((1, 2 * ROPE_DIM), lambda i: (0, 0)),
                  pl.BlockSpec((tm, 2 * ROPE_DIM), lambda i: (i % spt, 0))],
        out_specs=[pl.BlockSpec((1, nh, tm, HEAD_PAD), lambda i: (i // spt, 0, i % spt, 0)),
                   pl.BlockSpec((1, nh, 1, V_DIM, tm), lambda i: (i // spt, 0, i % spt, 0, 0))],
        out_shape=[jax.ShapeDtypeStruct((n_seq, nh, seq, HEAD_PAD), BF16),
                   jax.ShapeDtypeStruct((n_seq, nh, spt, V_DIM, tm), BF16)],
        compiler_params=_params("parallel"),
        name="kv_proj",
    )(zs, zs, ga, wk, wvt, gn, gr, tab)


def _attn_kernel(q_ref, k_ref, vt_ref, o_ref, s_even, s_odd, acc_scr):
    tk, tq = s_even.shape
    nk = vt_ref.shape[2]
    nq = q_ref.shape[2] // tq
    assert nk % 2 == 0 and nk >= 2

    def q_tile(i):
        return q_ref[0, 0, pl.ds(pl.multiple_of(i * tq, tq), tq), :]

    def scores(q, j, dst):
        k = k_ref[0, 0, pl.ds(pl.multiple_of(j * tk, tk), tk), :]
        dst[...] = lax.dot_general(k, q, (((1,), (1,)), ((), ())), preferred_element_type=F32)

    def update(j, src, m, l):
        st = src[...]
        m_new = jnp.maximum(m, jnp.max(st, axis=0, keepdims=True))
        alpha = jnp.exp2(m - m_new)
        p = jnp.exp2(st - m_new)
        l = alpha * l + jnp.sum(p, axis=0, keepdims=True)
        pv = jnp.dot(vt_ref[0, 0, j], p.astype(BF16), preferred_element_type=F32)
        acc_scr[...] = alpha * acc_scr[...] + pv
        return m_new, l

    scores(q_tile(0), 0, s_even)

    unroll = min(ATTN_UNROLL, nk)
    assert unroll % 2 == 0 and nk % unroll == 0
    bufs = (s_even, s_odd)

    def per_q_tile(i, _):
        q = q_tile(i)
        acc_scr[...] = jnp.zeros_like(acc_scr)

        def chunk_group(j0, carry, q_after):
            m, l = carry
            for u in range(unroll):
                if u + 1 < unroll:
                    scores(q, j0 + u + 1, bufs[(u + 1) % 2])
                else:
                    q_next, j_next = q_after
                    scores(q_next, j_next, bufs[0])
                m, l = update(j0 + u, bufs[u % 2], m, l)
            return m, l

        init = (jnp.full((1, tq), -jnp.inf, F32), jnp.zeros((1, tq), F32))
        carry = lax.fori_loop(
            0, nk // unroll - 1,
            lambda g, c: chunk_group(g * unroll, c, (q, (g + 1) * unroll)), init)
        m, l = chunk_group(nk - unroll, carry, (q_tile(jnp.minimum(i + 1, nq - 1)), 0))
        o_ref[0, pl.ds(pl.multiple_of(i * tq, tq), tq), :] = (acc_scr[...] / l).T.astype(o_ref.dtype)
        return 0

    lax.fori_loop(0, nq, per_q_tile, 0)


def _attention(q, k, vt, *, tq=512):
    n_seq, nh, seq, _ = q.shape
    nk, _, tk = vt.shape[2:]
    return pl.pallas_call(
        _attn_kernel,
        grid=(n_seq, nh),
        in_specs=[pl.BlockSpec((1, 1, seq, HEAD_PAD), lambda b, h: (b, h, 0, 0)),
                  pl.BlockSpec((1, 1, seq, HEAD_PAD), lambda b, h: (b, h, 0, 0)),
                  pl.BlockSpec((1, 1, nk, V_DIM, tk), lambda b, h: (b, h, 0, 0, 0))],
        out_specs=pl.BlockSpec((1, seq, V_DIM), lambda b, h: (b, 0, h)),
        out_shape=jax.ShapeDtypeStruct((n_seq, seq, nh * V_DIM), BF16),
        scratch_shapes=[pltpu.VMEM((tk, tq), F32), pltpu.VMEM((tk, tq), F32), pltpu.VMEM((V_DIM, tq), F32)],
        compiler_params=_params("parallel", "arbitrary"),
        name="attention",
    )(q, k, vt)


def _merge_kernel(a_ref, b_ref, wa_ref, wb_ref, ga_ref, gb_ref, o_ref):
    pa = jnp.dot(a_ref[...], wa_ref[...], preferred_element_type=F32)
    pb = jnp.dot(b_ref[...], wb_ref[...], preferred_element_type=F32)
    o_ref[...] = (ga_ref[...].astype(F32) * pa + gb_ref[...].astype(F32) * pb).astype(o_ref.dtype)


def _gated_merge(a, b, wa, wb, gates, *, tm=1024, tn=512):
    m, k = a.shape
    n = wa.shape[1]
    nj = n // tn
    return pl.pallas_call(
        _merge_kernel,
        grid=(m // tm, nj),
        in_specs=[pl.BlockSpec((tm, k), lambda i, j: (i, 0)),
                  pl.BlockSpec((tm, k), lambda i, j: (i, 0)),
                  pl.BlockSpec((k, tn), lambda i, j: (0, j)),
                  pl.BlockSpec((k, tn), lambda i, j: (0, j)),
                  pl.BlockSpec((tm, tn), lambda i, j: (i, j)),
                  pl.BlockSpec((tm, tn), lambda i, j: (i, j + nj))],
        out_specs=pl.BlockSpec((tm, tn), lambda i, j: (i, j)),
        out_shape=jax.ShapeDtypeStruct((m, n), BF16),
        compiler_params=_params("parallel", "arbitrary"),
        name="gated_merge",
    )(a, b, wa, wb, gates, gates)


def _ffn_up_kernel(h_ref, prev_ref, next_ref, wg_ref, wv_ref, cwg_ref, cwv_ref, cbg_ref, cbv_ref,
                   o_ref, a_scr, *, tm, seq):
    @pl.when(pl.program_id(1) == 0)
    def _():
        pos0 = (pl.program_id(0) * tm) % seq
        a_scr[:CONV_HALO] = jnp.where(pos0 > 0, prev_ref[...], jnp.zeros_like(prev_ref))
        a_scr[CONV_HALO:CONV_HALO + tm] = h_ref[...]
        a_scr[CONV_HALO + tm:] = jnp.where(pos0 + tm < seq, next_ref[...], jnp.zeros_like(next_ref))

    ext = tm + 2 * CONV_HALO
    chunk = ext // FFN_ROW_CHUNKS
    assert chunk * FFN_ROW_CHUNKS == ext and chunk % 16 == 0

    def conv(u, centre, n, cw_ref, cb_ref):
        cw = cw_ref[...]
        return (u[centre - 1:centre - 1 + n] * cw[0:1] + u[centre:centre + n] * cw[1:2]
                + u[centre + 1:centre + 1 + n] * cw[2:3] + cb_ref[...])

    ug = uv = None
    have_lo = 0
    done = 0
    for c in range(FFN_ROW_CHUNKS):
        a = a_scr[c * chunk:(c + 1) * chunk]
        ug_c = jnp.dot(a, wg_ref[...], preferred_element_type=F32)
        uv_c = jnp.dot(a, wv_ref[...], preferred_element_type=F32)
        ug = ug_c if ug is None else jnp.concatenate([ug, ug_c], axis=0)
        uv = uv_c if uv is None else jnp.concatenate([uv, uv_c], axis=0)
        upto = tm if c == FFN_ROW_CHUNKS - 1 else ((c + 1) * chunk - CONV_HALO - 1) // 16 * 16
        n = upto - done
        centre = done + CONV_HALO - have_lo
        gate = conv(ug, centre, n, cwg_ref, cbg_ref)
        val = conv(uv, centre, n, cwv_ref, cbv_ref)
        o_ref[done:upto] = (gate * _sigmoid(gate) * val).astype(o_ref.dtype)
        keep_lo = (upto + CONV_HALO - 1) // 8 * 8
        ug, uv = ug[keep_lo - have_lo:], uv[keep_lo - have_lo:]
        have_lo, done = keep_lo, upto


def _ffn_up(h, w_up, conv_w, conv_b, *, seq, tm=1024, tn=256):
    m, d = h.shape
    dff = w_up.shape[1] // 2
    nj = dff // tn
    hb = tm // CONV_HALO
    last = m // CONV_HALO - 1
    return pl.pallas_call(
        functools.partial(_ffn_up_kernel, tm=tm, seq=seq),
        grid=(m // tm, nj),
        in_specs=[pl.BlockSpec((tm, d), lambda i, j: (i, 0)),
                  pl.BlockSpec((CONV_HALO, d), lambda i, j: (jnp.maximum(i * hb - 1, 0), 0)),
                  pl.BlockSpec((CONV_HALO, d), lambda i, j: (jnp.minimum((i + 1) * hb, last), 0)),
                  pl.BlockSpec((d, tn), lambda i, j: (0, j)),
                  pl.BlockSpec((d, tn), lambda i, j: (0, j + nj)),
                  pl.BlockSpec((3, tn), lambda i, j: (0, j)),
                  pl.BlockSpec((3, tn), lambda i, j: (0, j + nj)),
                  pl.BlockSpec((1, tn), lambda i, j: (0, j)),
                  pl.BlockSpec((1, tn), lambda i, j: (0, j + nj))],
        out_specs=pl.BlockSpec((tm, tn), lambda i, j: (i, j)),
        out_shape=jax.ShapeDtypeStruct((m, dff), BF16),
        scratch_shapes=[pltpu.VMEM((tm + 2 * CONV_HALO, d), BF16)],
        compiler_params=_params("parallel", "arbitrary"),
        name="ffn_up",
    )(h, h, h, w_up, w_up, conv_w, conv_w, conv_b, conv_b)


def _swap_halves(a):
    h = a.shape[-1] // 2
    return jnp.concatenate([a[..., h:], a[..., :h]], axis=-1)


def _layer(x_parts, n_seq, seq, norm_mix_gain, w_in, pool_w, pool_scale, q_a_norm_gain, w_uq,
           kv_a_norm_gain, w_ukv, q_norm_gain, k_norm_gain, w_branch_pool, w_branch_mla, w_o,
           norm_ffn_gain, w_up, conv_w, conv_b, w_down):
    d = x_parts[0].shape[1]
    ng, pc, _ = pool_w.shape
    pool_width = ng * pc
    q_lora = q_a_norm_gain.shape[0]
    kv_lora = kv_a_norm_gain.shape[0]
    off_cq = pool_width
    off_ckv = off_cq + q_lora
    off_kr = off_ckv + kv_lora
    off_gp = off_kr + ROPE_DIM

    w_kr = w_in[:, off_kr:off_gp]
    small_cols = off_kr + 2 * ROPE_DIM
    small_pad = (-small_cols) % 768
    w_small = jnp.concatenate(
        [w_in[:, :off_kr], w_kr, _swap_halves(w_kr), jnp.zeros((d, small_pad), F32)], axis=1).astype(BF16)
    w_gate = w_in[:, off_gp:].astype(BF16)

    wq = w_uq.reshape(q_lora, N_HEADS, QK_DIM)
    wq = jnp.concatenate([wq, _swap_halves(wq[..., NOPE_DIM:])], axis=-1)
    wq = wq.transpose(1, 0, 2).astype(BF16)
    wkv = w_ukv.reshape(kv_lora, N_HEADS, NOPE_DIM + V_DIM).astype(BF16)
    wk = wkv[..., :NOPE_DIM].transpose(1, 0, 2)
    wvt = wkv[..., NOPE_DIM:].transpose(1, 2, 0)

    def gains(g):
        gr = g[NOPE_DIM:]
        return g[:NOPE_DIM].reshape(1, NOPE_DIM), jnp.concatenate([gr, _swap_halves(gr)]).reshape(1, 2 * ROPE_DIM)

    qgn, qgr = gains(q_norm_gain)
    kgn, kgr = gains(k_norm_gain)

    inv = 1.0 / (ROPE_THETA ** (jnp.arange(0, ROPE_DIM, 2, dtype=F32) / ROPE_DIM))
    ang = jnp.arange(seq, dtype=F32)[:, None] * inv[None, :]
    cos, sin = jnp.cos(ang), jnp.sin(ang)
    rope_tab = jnp.concatenate([cos, cos, -sin, sin], axis=1)

    h = _rmsnorm(x_parts, norm_mix_gain)
    zs = _matmul(h, w_small, tm=1024, tn=768, out_dtype=F32, name="in_proj_small")
    gates = _matmul(h, w_gate, tm=1024, tn=1024, out_dtype=BF16, gate=True, name="in_proj_gates")

    a_out = _pool_mixer(zs, pool_w.astype(BF16), pool_scale, seq=seq)

    q_scale = math.log2(math.e) / math.sqrt(QK_DIM)
    q = _q_proj(zs, off_cq // q_lora, q_a_norm_gain.reshape(1, q_lora), wq, qgn, qgr, rope_tab,
                n_seq=n_seq, seq=seq, out_scale=q_scale)
    k, vt = _kv_proj(zs, off_ckv // kv_lora, off_kr // (2 * ROPE_DIM), kv_a_norm_gain.reshape(1, kv_lora),
                     wk, wvt, kgn, kgr, rope_tab, n_seq=n_seq, seq=seq, tm=ATTN_KEY_TILE)
    b_out = _attention(q, k, vt).reshape(n_seq * seq, N_HEADS * V_DIM)

    merged = _gated_merge(a_out, b_out, w_branch_pool.astype(BF16), w_branch_mla.astype(BF16), gates)
    x1 = _matmul_residual(merged, w_o.astype(BF16), x_parts, tm=1024, tn=512, name="out_proj")

    h2 = _rmsnorm((x1,), norm_ffn_gain)
    act = _ffn_up(h2, w_up.astype(BF16), conv_w, conv_b.reshape(1, -1), seq=seq)
    return _matmul_residual(act, w_down.astype(BF16), (x1,), out_rows=tuple(p.shape[0] for p in x_parts),
                            tm=512, tn=512, name="down_proj")


def kernel(x_prompt, x_sample, norm_mix_gain, w_in, pool_w, pool_scale, q_a_norm_gain, w_uq,
           kv_a_norm_gain, w_ukv, q_norm_gain, k_norm_gain, w_branch_pool, w_branch_mla, w_o,
           norm_ffn_gain, w_up, conv_w, conv_b, w_down):
    bp, seq, d = x_prompt.shape
    bs = x_sample.shape[0]
    assert x_sample.shape[1:] == (seq, d)
    n_seq = bp + bs
    y = (x_prompt.reshape(bp * seq, d), x_sample.reshape(bs * seq, d))
    for l in range(norm_mix_gain.shape[0]):
        y = _layer(y, n_seq, seq, norm_mix_gain[l], w_in[l], pool_w[l], pool_scale[l], q_a_norm_gain[l],
                   w_uq[l], kv_a_norm_gain[l], w_ukv[l], q_norm_gain[l], k_norm_gain[l],
                   w_branch_pool[l], w_branch_mla[l], w_o[l], norm_ffn_gain[l], w_up[l],
                   conv_w[l], conv_b[l], w_down[l])
    return (y[0].reshape(bp, seq, d), y[1].reshape(bs, seq, d))
```

```python
import functools
import math

import jax
import jax.numpy as jnp
from jax import lax
from jax.experimental import pallas as pl
from jax.experimental.pallas import tpu as pltpu

F32 = jnp.float32
BF16 = jnp.bfloat16

RMS_EPS = 1e-6
ROPE_THETA = 10000.0
N_HEADS = 16
NOPE_DIM = 128
ROPE_DIM = 64
V_DIM = 128
QK_DIM = NOPE_DIM + ROPE_DIM
HEAD_PAD = 256
V_ROWS = V_DIM + 16
POOL_WINDOWS = (2, 4, 8, 16)
POOL_HALO = 8
CONV_HALO = 16
FFN_ROW_CHUNKS = 2
ATTN_UNROLL = 16
ATTN_KEY_TILE = 512

V7X_VMEM_LIMIT = 56 * 1024 * 1024


def _params(*sem):
    return pltpu.CompilerParams(dimension_semantics=sem, vmem_limit_bytes=V7X_VMEM_LIMIT)


def _stack_tiles(parts, tm):
    counts = [p.shape[0] // tm for p in parts]
    offsets = [sum(counts[:k]) for k in range(len(parts))]
    return counts, offsets


def _stack_spec(block, count, offset, col=None, ncols=1):
    def index(i, *j):
        row = jnp.clip(i - offset, 0, count - 1)
        if col is None:
            return row, 0
        active = (i >= offset) & (i < offset + count)
        frozen = jnp.where(i < offset, 0, ncols - 1)
        return row, jnp.where(active, col(*j), frozen)
    return pl.BlockSpec(block, index)


def _stack_read(refs, counts, offsets, i):
    x = refs[-1][...]
    for k in range(len(refs) - 2, -1, -1):
        x = jnp.where(i < offsets[k] + counts[k], refs[k][...], x)
    return x


def _rmsnorm_kernel(*refs, counts, offsets):
    *x_refs, g_ref, o_ref = refs
    x = _stack_read(x_refs, counts, offsets, pl.program_id(0))
    ms = jnp.mean(x * x, axis=-1, keepdims=True)
    o_ref[...] = (x * lax.rsqrt(ms + RMS_EPS) * g_ref[...]).astype(o_ref.dtype)


def _rmsnorm(parts, gain, tr=256):
    d = parts[0].shape[1]
    counts, offsets = _stack_tiles(parts, tr)
    return pl.pallas_call(
        functools.partial(_rmsnorm_kernel, counts=counts, offsets=offsets),
        grid=(sum(counts),),
        in_specs=[_stack_spec((tr, d), c, o) for c, o in zip(counts, offsets)]
                 + [pl.BlockSpec((1, d), lambda i: (0, 0))],
        out_specs=pl.BlockSpec((tr, d), lambda i: (i, 0)),
        out_shape=jax.ShapeDtypeStruct((sum(counts) * tr, d), BF16),
        compiler_params=_params("arbitrary"),
        name="rmsnorm",
    )(*parts, gain.reshape(1, d))


def _sigmoid(x):
    return 0.5 * jnp.tanh(0.5 * x) + 0.5


def _mm_kernel(a_ref, b_ref, o_ref, *, gate):
    acc = jnp.dot(a_ref[...], b_ref[...], preferred_element_type=F32)
    if gate:
        acc = _sigmoid(acc)
    o_ref[...] = acc.astype(o_ref.dtype)


def _matmul(a, b, *, tm, tn, out_dtype, gate=False, name):
    m, k = a.shape
    _, n = b.shape
    return pl.pallas_call(
        functools.partial(_mm_kernel, gate=gate),
        grid=(m // tm, n // tn),
        in_specs=[pl.BlockSpec((tm, k), lambda i, j: (i, 0)),
                  pl.BlockSpec((k, tn), lambda i, j: (0, j))],
        out_specs=pl.BlockSpec((tm, tn), lambda i, j: (i, j)),
        out_shape=jax.ShapeDtypeStruct((m, n), out_dtype),
        compiler_params=_params("parallel", "arbitrary"),
        name=name,
    )(a, b)


def _mm_res_kernel(*refs, n_res, res_tiles, out_tiles):
    a_ref, b_ref = refs[:2]
    r_refs, o_refs = refs[2:2 + n_res], refs[2 + n_res:]
    i = pl.program_id(0)
    acc = jnp.dot(a_ref[...], b_ref[...], preferred_element_type=F32)
    y = _stack_read(r_refs, *res_tiles, i) + acc
    if len(o_refs) == 1:
        o_refs[0][...] = y
    else:
        for o_ref, count, offset in zip(o_refs, *out_tiles):
            @pl.when((i >= offset) & (i < offset + count))
            def _(o_ref=o_ref):
                o_ref[...] = y


def _matmul_residual(a, b, res_parts, *, out_rows=None, tm, tn, name):
    m, k = a.shape
    _, n = b.shape
    nj = n // tn
    res_tiles = _stack_tiles(res_parts, tm)
    out_rows = out_rows or (m,)
    out_tiles = [r // tm for r in out_rows], [sum(out_rows[:q]) // tm for q in range(len(out_rows))]
    col = lambda j: j
    if len(out_rows) == 1:
        out_specs = pl.BlockSpec((tm, tn), lambda i, j: (i, j))
        out_shape = jax.ShapeDtypeStruct((m, n), F32)
    else:
        out_specs = [_stack_spec((tm, tn), c, o, col, nj) for c, o in zip(*out_tiles)]
        out_shape = [jax.ShapeDtypeStruct((r, n), F32) for r in out_rows]
    return pl.pallas_call(
        functools.partial(_mm_res_kernel, n_res=len(res_parts), res_tiles=res_tiles, out_tiles=out_tiles),
        grid=(m // tm, nj),
        in_specs=[pl.BlockSpec((tm, k), lambda i, j: (i, 0)),
                  pl.BlockSpec((k, tn), lambda i, j: (0, j))]
                 + [_stack_spec((tm, tn), c, o, col, nj) for c, o in zip(*res_tiles)],
        out_specs=out_specs,
        out_shape=out_shape,
        compiler_params=_params("arbitrary", "arbitrary"),
        name=name,
    )(a, b, *res_parts)


def _pool_kernel(u_ref, prev_ref, next_ref, w_ref, scale_ref, o_ref, *, tm, seq):
    i = pl.program_id(0)
    g = pl.program_id(1)
    half = jnp.left_shift(1, g)
    pos0 = (i * tm) % seq
    u = u_ref[...]
    prev = jnp.where(pos0 > 0, prev_ref[...], 0.0)
    nxt = jnp.where(pos0 + tm < seq, next_ref[...], 0.0)
    ext = jnp.concatenate([prev, u, nxt], axis=0)
    rr = lax.broadcasted_iota(jnp.int32, (tm, tm + 2 * POOL_HALO), 0)
    cc = lax.broadcasted_iota(jnp.int32, (tm, tm + 2 * POOL_HALO), 1) - POOL_HALO
    band = jnp.where((cc >= rr - half) & (cc < rr + half), 1.0, 0.0).astype(BF16)
    hi = ext.astype(BF16)
    lo = (ext - hi.astype(F32)).astype(BF16)
    wsum = (jnp.dot(band, hi, preferred_element_type=F32)
            + jnp.dot(band, lo, preferred_element_type=F32))
    t = pos0 + lax.broadcasted_iota(jnp.int32, (tm, 1), 0)
    cnt = (jnp.minimum(t + half, seq) - jnp.maximum(t - half, 0)).astype(F32)
    pooled = wsum / cnt - u
    y = jnp.dot(pooled.astype(BF16), w_ref[0], preferred_element_type=F32)
    o_ref[...] = (y * scale_ref[...]).astype(o_ref.dtype)


def _pool_mixer(zs, pool_w, pool_scale, *, seq, tm=512):
    m = zs.shape[0]
    ng, c, _ = pool_w.shape
    hb = tm // POOL_HALO
    last = m // POOL_HALO - 1
    return pl.pallas_call(
        functools.partial(_pool_kernel, tm=tm, seq=seq),
        grid=(m // tm, ng),
        in_specs=[pl.BlockSpec((tm, c), lambda i, g: (i, g)),
                  pl.BlockSpec((POOL_HALO, c), lambda i, g: (jnp.maximum(i * hb - 1, 0), g)),
                  pl.BlockSpec((POOL_HALO, c), lambda i, g: (jnp.minimum((i + 1) * hb, last), g)),
                  pl.BlockSpec((1, c, c), lambda i, g: (g, 0, 0)),
                  pl.BlockSpec((1, c), lambda i, g: (0, g))],
        out_specs=pl.BlockSpec((tm, c), lambda i, g: (i, g)),
        out_shape=jax.ShapeDtypeStruct((m, ng * c), BF16),
        compiler_params=_params("parallel", "arbitrary"),
        name="pool_mixer",
    )(zs, zs, zs, pool_w, pool_scale.reshape(1, ng * c))


def _fold_rope(t):
    folded = t + pltpu.roll(t, ROPE_DIM, axis=1)
    lane = lax.broadcasted_iota(jnp.int32, t.shape, 1)
    return jnp.where(lane < ROPE_DIM, folded, 0.0)


def _qproj_kernel(c_ref, ga_ref, w_ref, gn_ref, gr_ref, tab_ref, o_ref, *, out_scale):
    x = c_ref[...]
    ms = jnp.mean(x * x, axis=-1, keepdims=True)
    a = (x * lax.rsqrt(ms + RMS_EPS) * ga_ref[...]).astype(BF16)
    gn = gn_ref[...] * out_scale
    rope_mul = gr_ref[...] * tab_ref[...] * out_scale
    for h in range(w_ref.shape[0]):
        y = jnp.dot(a, w_ref[h], preferred_element_type=F32)
        nope = y[:, :NOPE_DIM]
        rr = y[:, NOPE_DIM:]
        ssq = jnp.sum(nope * nope, axis=-1, keepdims=True) + 0.5 * jnp.sum(rr * rr, axis=-1, keepdims=True)
        r = lax.rsqrt(ssq * (1.0 / QK_DIM) + RMS_EPS)
        o_ref[0, h, :, :NOPE_DIM] = (nope * r * gn).astype(o_ref.dtype)
        o_ref[0, h, :, NOPE_DIM:] = _fold_rope(rr * r * rope_mul).astype(o_ref.dtype)


def _q_proj(zs, col_block, ga, wq, gn, gr, tab, *, n_seq, seq, out_scale, tm=512):
    m = zs.shape[0]
    nh, kq, _ = wq.shape
    spt = seq // tm
    return pl.pallas_call(
        functools.partial(_qproj_kernel, out_scale=out_scale),
        grid=(m // tm,),
        in_specs=[pl.BlockSpec((tm, kq), lambda i: (i, col_block)),
                  pl.BlockSpec((1, kq), lambda i: (0, 0)),
                  pl.BlockSpec((nh, kq, HEAD_PAD), lambda i: (0, 0, 0)),
                  pl.BlockSpec((1, NOPE_DIM), lambda i: (0, 0)),
                  pl.BlockSpec((1, 2 * ROPE_DIM), lambda i: (0, 0)),
                  pl.BlockSpec((tm, 2 * ROPE_DIM), lambda i: (i % spt, 0))],
        out_specs=pl.BlockSpec((1, nh, tm, HEAD_PAD), lambda i: (i // spt, 0, i % spt, 0)),
        out_shape=jax.ShapeDtypeStruct((n_seq, nh, seq, HEAD_PAD), BF16),
        compiler_params=_params("parallel"),
        name="q_proj",
    )(zs, ga, wq, gn, gr, tab)


def _kvproj_kernel(c_ref, kr_ref, ga_ref, wk_ref, wvt_ref, gn_ref, gr_ref, tab_ref, k_ref, vt_ref):
    x = c_ref[...]
    ms = jnp.mean(x * x, axis=-1, keepdims=True)
    a = (x * lax.rsqrt(ms + RMS_EPS) * ga_ref[...]).astype(BF16)
    kr = kr_ref[...]
    ssq_rope = 0.5 * jnp.sum(kr * kr, axis=-1, keepdims=True)
    rope = _fold_rope(kr * gr_ref[...] * tab_ref[...])
    gn = gn_ref[...]
    pad_shape = (vt_ref.shape[3] - V_DIM, vt_ref.shape[4])
    ones_rows = jnp.where(lax.broadcasted_iota(jnp.int32, pad_shape, 0) == 0, 1.0, 0.0).astype(vt_ref.dtype)
    for h in range(wk_ref.shape[0]):
        nope = jnp.dot(a, wk_ref[h], preferred_element_type=F32)
        ssq = jnp.sum(nope * nope, axis=-1, keepdims=True) + ssq_rope
        r = lax.rsqrt(ssq * (1.0 / QK_DIM) + RMS_EPS)
        k_ref[0, h, :, :NOPE_DIM] = (nope * r * gn).astype(k_ref.dtype)
        k_ref[0, h, :, NOPE_DIM:] = (rope * r).astype(k_ref.dtype)
        vt = lax.dot_general(wvt_ref[h], a, (((1,), (1,)), ((), ())), preferred_element_type=F32)
        vt_ref[0, h, 0, :V_DIM] = vt.astype(vt_ref.dtype)
        vt_ref[0, h, 0, V_DIM:] = ones_rows


def _kv_proj(zs, ckv_block, kr_block, ga, wk, wvt, gn, gr, tab, *, n_seq, seq, tm):
    m = zs.shape[0]
    nh, kc, _ = wk.shape
    spt = seq // tm
    return pl.pallas_call(
        _kvproj_kernel,
        grid=(m // tm,),
        in_specs=[pl.BlockSpec((tm, kc), lambda i: (i, ckv_block)),
                  pl.BlockSpec((tm, 2 * ROPE_DIM), lambda i: (i, kr_block)),
                  pl.BlockSpec((1, kc), lambda i: (0, 0)),
                  pl.BlockSpec((nh, kc, NOPE_DIM), lambda i: (0, 0, 0)),
                  pl.BlockSpec((nh, V_DIM, kc), lambda i: (0, 0, 0)),
                  pl.BlockSpec((1, NOPE_DIM), lambda i: (0, 0)),
                  pl.BlockSpec((1, 2 * ROPE_DIM), lambda i: (0, 0)),
                  pl.BlockSpec((tm, 2 * ROPE_DIM), lambda i: (i % spt, 0))],
        out_specs=[pl.BlockSpec((1, nh, tm, HEAD_PAD), lambda i: (i // spt, 0, i % spt, 0)),
                   pl.BlockSpec((1, nh, 1, V_ROWS, tm), lambda i: (i // spt, 0, i % spt, 0, 0))],
        out_shape=[jax.ShapeDtypeStruct((n_seq, nh, seq, HEAD_PAD), BF16),
                   jax.ShapeDtypeStruct((n_seq, nh, spt, V_ROWS, tm), BF16)],
        compiler_params=_params("parallel"),
        name="kv_proj",
    )(zs, zs, ga, wk, wvt, gn, gr, tab)


def _attn_kernel(q_ref, k_ref, vt_ref, o_ref, s_even, s_odd, acc_scr):
    tk, tq = s_even.shape
    nk = vt_ref.shape[2]
    nq = q_ref.shape[2] // tq
    assert nk % 2 == 0 and nk >= 2

    def q_tile(i):
        return q_ref[0, 0, pl.ds(pl.multiple_of(i * tq, tq), tq), :]

    def scores(q, j, dst):
        k = k_ref[0, 0, pl.ds(pl.multiple_of(j * tk, tk), tk), :]
        dst[...] = lax.dot_general(k, q, (((1,), (1,)), ((), ())), preferred_element_type=F32)

    def update(j, src, m):
        st = src[...]
        m_new = jnp.maximum(m, jnp.max(st, axis=0, keepdims=True))
        alpha = jnp.exp2(m - m_new)
        p = jnp.exp2(st - m_new).astype(BF16)
        pv = jnp.dot(vt_ref[0, 0, j], p, preferred_element_type=F32)
        acc_scr[...] = alpha * acc_scr[...] + pv
        return m_new

    scores(q_tile(0), 0, s_even)

    unroll = min(ATTN_UNROLL, nk)
    assert unroll % 2 == 0 and nk % unroll == 0
    bufs = (s_even, s_odd)

    def per_q_tile(i, _):
        q = q_tile(i)
        acc_scr[...] = jnp.zeros_like(acc_scr)

        def chunk_group(j0, m, q_after):
            for u in range(unroll):
                if u + 1 < unroll:
                    scores(q, j0 + u + 1, bufs[(u + 1) % 2])
                else:
                    q_next, j_next = q_after
                    scores(q_next, j_next, bufs[0])
                m = update(j0 + u, bufs[u % 2], m)
            return m

        m = lax.fori_loop(
            0, nk // unroll - 1,
            lambda g, c: chunk_group(g * unroll, c, (q, (g + 1) * unroll)),
            jnp.full((1, tq), -jnp.inf, F32))
        chunk_group(nk - unroll, m, (q_tile(jnp.minimum(i + 1, nq - 1)), 0))
        out = acc_scr[:V_DIM] / acc_scr[V_DIM:V_DIM + 1]
        o_ref[0, pl.ds(pl.multiple_of(i * tq, tq), tq), :] = out.T.astype(o_ref.dtype)
        return 0

    lax.fori_loop(0, nq, per_q_tile, 0)


def _attention(q, k, vt, *, tq=512):
    n_seq, nh, seq, _ = q.shape
    nk, v_rows, tk = vt.shape[2:]
    return pl.pallas_call(
        _attn_kernel,
        grid=(n_seq, nh),
        in_specs=[pl.BlockSpec((1, 1, seq, HEAD_PAD), lambda b, h: (b, h, 0, 0)),
                  pl.BlockSpec((1, 1, seq, HEAD_PAD), lambda b, h: (b, h, 0, 0)),
                  pl.BlockSpec((1, 1, nk, v_rows, tk), lambda b, h: (b, h, 0, 0, 0))],
        out_specs=pl.BlockSpec((1, seq, V_DIM), lambda b, h: (b, 0, h)),
        out_shape=jax.ShapeDtypeStruct((n_seq, seq, nh * V_DIM), BF16),
        scratch_shapes=[pltpu.VMEM((tk, tq), F32), pltpu.VMEM((tk, tq), F32), pltpu.VMEM((v_rows, tq), F32)],
        compiler_params=_params("parallel", "arbitrary"),
        name="attention",
    )(q, k, vt)


def _merge_kernel(a_ref, b_ref, wa_ref, wb_ref, ga_ref, gb_ref, o_ref):
    pa = jnp.dot(a_ref[...], wa_ref[...], preferred_element_type=F32)
    pb = jnp.dot(b_ref[...], wb_ref[...], preferred_element_type=F32)
    o_ref[...] = (ga_ref[...].astype(F32) * pa + gb_ref[...].astype(F32) * pb).astype(o_ref.dtype)


def _gated_merge(a, b, wa, wb, gates, *, tm=1024, tn=1024):
    m, k = a.shape
    n = wa.shape[1]
    nj = n // tn
    return pl.pallas_call(
        _merge_kernel,
        grid=(m // tm, nj),
        in_specs=[pl.BlockSpec((tm, k), lambda i, j: (i, 0)),
                  pl.BlockSpec((tm, k), lambda i, j: (i, 0)),
                  pl.BlockSpec((k, tn), lambda i, j: (0, j)),
                  pl.BlockSpec((k, tn), lambda i, j: (0, j)),
                  pl.BlockSpec((tm, tn), lambda i, j: (i, j)),
                  pl.BlockSpec((tm, tn), lambda i, j: (i, j + nj))],
        out_specs=pl.BlockSpec((tm, tn), lambda i, j: (i, j)),
        out_shape=jax.ShapeDtypeStruct((m, n), BF16),
        compiler_params=_params("parallel", "arbitrary"),
        name="gated_merge",
    )(a, b, wa, wb, gates, gates)


def _ffn_up_kernel(h_ref, prev_ref, next_ref, wg_ref, wv_ref, cwg_ref, cwv_ref, cbg_ref, cbv_ref,
                   o_ref, a_scr, *, tm, seq):
    @pl.when(pl.program_id(1) == 0)
    def _():
        pos0 = (pl.program_id(0) * tm) % seq
        a_scr[:CONV_HALO] = jnp.where(pos0 > 0, prev_ref[...], jnp.zeros_like(prev_ref))
        a_scr[CONV_HALO:CONV_HALO + tm] = h_ref[...]
        a_scr[CONV_HALO + tm:] = jnp.where(pos0 + tm < seq, next_ref[...], jnp.zeros_like(next_ref))

    ext = tm + 2 * CONV_HALO
    chunk = ext // FFN_ROW_CHUNKS
    assert chunk * FFN_ROW_CHUNKS == ext and chunk % 16 == 0

    def conv(u, centre, n, cw_ref, cb_ref):
        cw = cw_ref[...]
        rows = u.shape[0]
        before = pltpu.roll(u, 1, 0)[centre:centre + n]
        after = pltpu.roll(u, rows - 1, 0)[centre:centre + n]
        return before * cw[0:1] + u[centre:centre + n] * cw[1:2] + after * cw[2:3] + cb_ref[...]

    ug = uv = None
    have_lo = 0
    done = 0
    for c in range(FFN_ROW_CHUNKS):
        a = a_scr[c * chunk:(c + 1) * chunk]
        ug_c = jnp.dot(a, wg_ref[...], preferred_element_type=F32)
        uv_c = jnp.dot(a, wv_ref[...], preferred_element_type=F32)
        ug = ug_c if ug is None else jnp.concatenate([ug, ug_c], axis=0)
        uv = uv_c if uv is None else jnp.concatenate([uv, uv_c], axis=0)
        upto = tm if c == FFN_ROW_CHUNKS - 1 else ((c + 1) * chunk - CONV_HALO - 1) // 16 * 16
        n = upto - done
        centre = done + CONV_HALO - have_lo
        gate = conv(ug, centre, n, cwg_ref, cbg_ref)
        val = conv(uv, centre, n, cwv_ref, cbv_ref)
        half_gate = 0.5 * gate
        o_ref[done:upto] = ((half_gate * jnp.tanh(half_gate) + half_gate) * val).astype(o_ref.dtype)
        keep_lo = (upto + CONV_HALO - 1) // 8 * 8
        ug, uv = ug[keep_lo - have_lo:], uv[keep_lo - have_lo:]
        have_lo, done = keep_lo, upto


def _ffn_up(h, w_up, conv_w, conv_b, *, seq, tm=1024, tn=256):
    m, d = h.shape
    dff = w_up.shape[1] // 2
    nj = dff // tn
    hb = tm // CONV_HALO
    last = m // CONV_HALO - 1
    return pl.pallas_call(
        functools.partial(_ffn_up_kernel, tm=tm, seq=seq),
        grid=(m // tm, nj),
        in_specs=[pl.BlockSpec((tm, d), lambda i, j: (i, 0)),
                  pl.BlockSpec((CONV_HALO, d), lambda i, j: (jnp.maximum(i * hb - 1, 0), 0)),
                  pl.BlockSpec((CONV_HALO, d), lambda i, j: (jnp.minimum((i + 1) * hb, last), 0)),
                  pl.BlockSpec((d, tn), lambda i, j: (0, j)),
                  pl.BlockSpec((d, tn), lambda i, j: (0, j + nj)),
                  pl.BlockSpec((3, tn), lambda i, j: (0, j)),
                  pl.BlockSpec((3, tn), lambda i, j: (0, j + nj)),
                  pl.BlockSpec((1, tn), lambda i, j: (0, j)),
                  pl.BlockSpec((1, tn), lambda i, j: (0, j + nj))],
        out_specs=pl.BlockSpec((tm, tn), lambda i, j: (i, j)),
        out_shape=jax.ShapeDtypeStruct((m, dff), BF16),
        scratch_shapes=[pltpu.VMEM((tm + 2 * CONV_HALO, d), BF16)],
        compiler_params=_params("parallel", "arbitrary"),
        name="ffn_up",
    )(h, h, h, w_up, w_up, conv_w, conv_w, conv_b, conv_b)


def _swap_halves(a):
    h = a.shape[-1] // 2
    return jnp.concatenate([a[..., h:], a[..., :h]], axis=-1)


def _layer(x_parts, n_seq, seq, norm_mix_gain, w_in, pool_w, pool_scale, q_a_norm_gain, w_uq,
           kv_a_norm_gain, w_ukv, q_norm_gain, k_norm_gain, w_branch_pool, w_branch_mla, w_o,
           norm_ffn_gain, w_up, conv_w, conv_b, w_down):
    d = x_parts[0].shape[1]
    ng, pc, _ = pool_w.shape
    pool_width = ng * pc
    q_lora = q_a_norm_gain.shape[0]
    kv_lora = kv_a_norm_gain.shape[0]
    off_cq = pool_width
    off_ckv = off_cq + q_lora
    off_kr = off_ckv + kv_lora
    off_gp = off_kr + ROPE_DIM

    w_kr = w_in[:, off_kr:off_gp]
    small_cols = off_kr + 2 * ROPE_DIM
    small_pad = (-small_cols) % 768
    w_small = jnp.concatenate(
        [w_in[:, :off_kr], w_kr, _swap_halves(w_kr), jnp.zeros((d, small_pad), F32)], axis=1).astype(BF16)
    w_gate = w_in[:, off_gp:].astype(BF16)

    wq = w_uq.reshape(q_lora, N_HEADS, QK_DIM)
    wq = jnp.concatenate([wq, _swap_halves(wq[..., NOPE_DIM:])], axis=-1)
    wq = wq.transpose(1, 0, 2).astype(BF16)
    wkv = w_ukv.reshape(kv_lora, N_HEADS, NOPE_DIM + V_DIM).astype(BF16)
    wk = wkv[..., :NOPE_DIM].transpose(1, 0, 2)
    wvt = wkv[..., NOPE_DIM:].transpose(1, 2, 0)

    def gains(g):
        gr = g[NOPE_DIM:]
        return g[:NOPE_DIM].reshape(1, NOPE_DIM), jnp.concatenate([gr, _swap_halves(gr)]).reshape(1, 2 * ROPE_DIM)

    qgn, qgr = gains(q_norm_gain)
    kgn, kgr = gains(k_norm_gain)

    inv = 1.0 / (ROPE_THETA ** (jnp.arange(0, ROPE_DIM, 2, dtype=F32) / ROPE_DIM))
    ang = jnp.arange(seq, dtype=F32)[:, None] * inv[None, :]
    cos, sin = jnp.cos(ang), jnp.sin(ang)
    rope_tab = jnp.concatenate([cos, cos, -sin, sin], axis=1)

    h = _rmsnorm(x_parts, norm_mix_gain)
    zs = _matmul(h, w_small, tm=1024, tn=768, out_dtype=F32, name="in_proj_small")
    gates = _matmul(h, w_gate, tm=1024, tn=1024, out_dtype=BF16, gate=True, name="in_proj_gates")

    a_out = _pool_mixer(zs, pool_w.astype(BF16), pool_scale, seq=seq)

    q_scale = math.log2(math.e) / math.sqrt(QK_DIM)
    q = _q_proj(zs, off_cq // q_lora, q_a_norm_gain.reshape(1, q_lora), wq, qgn, qgr, rope_tab,
                n_seq=n_seq, seq=seq, out_scale=q_scale)
    k, vt = _kv_proj(zs, off_ckv // kv_lora, off_kr // (2 * ROPE_DIM), kv_a_norm_gain.reshape(1, kv_lora),
                     wk, wvt, kgn, kgr, rope_tab, n_seq=n_seq, seq=seq, tm=ATTN_KEY_TILE)
    b_out = _attention(q, k, vt).reshape(n_seq * seq, N_HEADS * V_DIM)

    merged = _gated_merge(a_out, b_out, w_branch_pool.astype(BF16), w_branch_mla.astype(BF16), gates)
    x1 = _matmul_residual(merged, w_o.astype(BF16), x_parts, tm=1024, tn=512, name="out_proj")

    h2 = _rmsnorm((x1,), norm_ffn_gain)
    act = _ffn_up(h2, w_up.astype(BF16), conv_w, conv_b.reshape(1, -1), seq=seq)
    return _matmul_residual(act, w_down.astype(BF16), (x1,), out_rows=tuple(p.shape[0] for p in x_parts),
                            tm=512, tn=512, name="down_proj")


def kernel(x_prompt, x_sample, norm_mix_gain, w_in, pool_w, pool_scale, q_a_norm_gain, w_uq,
           kv_a_norm_gain, w_ukv, q_norm_gain, k_norm_gain, w_branch_pool, w_branch_mla, w_o,
           norm_ffn_gain, w_up, conv_w, conv_b, w_down):
    bp, seq, d = x_prompt.shape
    bs = x_sample.shape[0]
    assert x_sample.shape[1:] == (seq, d)
    n_seq = bp + bs
    y = (x_prompt.reshape(bp * seq, d), x_sample.reshape(bs * seq, d))
    for l in range(norm_mix_gain.shape[0]):
        y = _layer(y, n_seq, seq, norm_mix_gain[l], w_in[l], pool_w[l], pool_scale[l], q_a_norm_gain[l],
                   w_uq[l], kv_a_norm_gain[l], w_ukv[l], q_norm_gain[l], k_norm_gain[l],
                   w_branch_pool[l], w_branch_mla[l], w_o[l], norm_ffn_gain[l], w_up[l],
                   conv_w[l], conv_b[l], w_down[l])
    return (y[0].reshape(bp, seq, d), y[1].reshape(bs, seq, d))
```

```python
import functools
import math

import jax
import jax.numpy as jnp
from jax import lax
from jax.experimental import pallas as pl
from jax.experimental.pallas import tpu as pltpu

F32 = jnp.float32
BF16 = jnp.bfloat16

RMS_EPS = 1e-6
ROPE_THETA = 10000.0
N_HEADS = 16
NOPE_DIM = 128
ROPE_DIM = 64
V_DIM = 128
QK_DIM = NOPE_DIM + ROPE_DIM
HEAD_PAD = 256
V_ROWS = V_DIM + 16
POOL_WINDOWS = (2, 4, 8, 16)
POOL_HALO = 8
CONV_HALO = 16
FFN_NORM_ROWS = 128
FFN_ROW_CHUNKS = 2
ATTN_UNROLL = 16
ATTN_KEY_TILE = 512

V7X_VMEM_LIMIT = 56 * 1024 * 1024


def _params(*sem):
    return pltpu.CompilerParams(dimension_semantics=sem, vmem_limit_bytes=V7X_VMEM_LIMIT)


def _stack_tiles(parts, tm):
    counts = [p.shape[0] // tm for p in parts]
    offsets = [sum(counts[:k]) for k in range(len(parts))]
    return counts, offsets


def _stack_spec(block, count, offset, col=None, ncols=1):
    def index(i, *j):
        row = jnp.clip(i - offset, 0, count - 1)
        if col is None:
            return row, 0
        active = (i >= offset) & (i < offset + count)
        frozen = jnp.where(i < offset, 0, ncols - 1)
        return row, jnp.where(active, col(*j), frozen)
    return pl.BlockSpec(block, index)


def _stack_read(refs, counts, offsets, i):
    x = refs[-1][...]
    for k in range(len(refs) - 2, -1, -1):
        x = jnp.where(i < offsets[k] + counts[k], refs[k][...], x)
    return x


def _rmsnorm_kernel(*refs, counts, offsets):
    *x_refs, g_ref, o_ref = refs
    x = _stack_read(x_refs, counts, offsets, pl.program_id(0))
    ms = jnp.mean(x * x, axis=-1, keepdims=True)
    o_ref[...] = (x * lax.rsqrt(ms + RMS_EPS) * g_ref[...]).astype(o_ref.dtype)


def _rmsnorm(parts, gain, tr=256):
    d = parts[0].shape[1]
    counts, offsets = _stack_tiles(parts, tr)
    return pl.pallas_call(
        functools.partial(_rmsnorm_kernel, counts=counts, offsets=offsets),
        grid=(sum(counts),),
        in_specs=[_stack_spec((tr, d), c, o) for c, o in zip(counts, offsets)]
                 + [pl.BlockSpec((1, d), lambda i: (0, 0))],
        out_specs=pl.BlockSpec((tr, d), lambda i: (i, 0)),
        out_shape=jax.ShapeDtypeStruct((sum(counts) * tr, d), BF16),
        compiler_params=_params("arbitrary"),
        name="rmsnorm",
    )(*parts, gain.reshape(1, d))


def _sigmoid(x):
    return 0.5 * jnp.tanh(0.5 * x) + 0.5


def _mm_kernel(a_ref, b_ref, o_ref, *, gate):
    acc = jnp.dot(a_ref[...], b_ref[...], preferred_element_type=F32)
    if gate:
        acc = _sigmoid(acc)
    o_ref[...] = acc.astype(o_ref.dtype)


def _matmul(a, b, *, tm, tn, out_dtype, gate=False, name):
    m, k = a.shape
    _, n = b.shape
    return pl.pallas_call(
        functools.partial(_mm_kernel, gate=gate),
        grid=(m // tm, n // tn),
        in_specs=[pl.BlockSpec((tm, k), lambda i, j: (i, 0)),
                  pl.BlockSpec((k, tn), lambda i, j: (0, j))],
        out_specs=pl.BlockSpec((tm, tn), lambda i, j: (i, j)),
        out_shape=jax.ShapeDtypeStruct((m, n), out_dtype),
        compiler_params=_params("parallel", "arbitrary"),
        name=name,
    )(a, b)


def _mm_res_kernel(*refs, n_res, res_tiles, out_tiles):
    a_ref, b_ref = refs[:2]
    r_refs, o_refs = refs[2:2 + n_res], refs[2 + n_res:]
    i = pl.program_id(0)
    acc = jnp.dot(a_ref[...], b_ref[...], preferred_element_type=F32)
    y = _stack_read(r_refs, *res_tiles, i) + acc
    if len(o_refs) == 1:
        o_refs[0][...] = y
    else:
        for o_ref, count, offset in zip(o_refs, *out_tiles):
            @pl.when((i >= offset) & (i < offset + count))
            def _(o_ref=o_ref):
                o_ref[...] = y


def _matmul_residual(a, b, res_parts, *, out_rows=None, tm, tn, name):
    m, k = a.shape
    _, n = b.shape
    nj = n // tn
    res_tiles = _stack_tiles(res_parts, tm)
    out_rows = out_rows or (m,)
    out_tiles = [r // tm for r in out_rows], [sum(out_rows[:q]) // tm for q in range(len(out_rows))]
    col = lambda j: j
    if len(out_rows) == 1:
        out_specs = pl.BlockSpec((tm, tn), lambda i, j: (i, j))
        out_shape = jax.ShapeDtypeStruct((m, n), F32)
    else:
        out_specs = [_stack_spec((tm, tn), c, o, col, nj) for c, o in zip(*out_tiles)]
        out_shape = [jax.ShapeDtypeStruct((r, n), F32) for r in out_rows]
    return pl.pallas_call(
        functools.partial(_mm_res_kernel, n_res=len(res_parts), res_tiles=res_tiles, out_tiles=out_tiles),
        grid=(m // tm, nj),
        in_specs=[pl.BlockSpec((tm, k), lambda i, j: (i, 0)),
                  pl.BlockSpec((k, tn), lambda i, j: (0, j))]
                 + [_stack_spec((tm, tn), c, o, col, nj) for c, o in zip(*res_tiles)],
        out_specs=out_specs,
        out_shape=out_shape,
        compiler_params=_params("arbitrary", "arbitrary"),
        name=name,
    )(a, b, *res_parts)


def _pool_kernel(u_ref, prev_ref, next_ref, w_ref, scale_ref, o_ref, *, tm, seq):
    i = pl.program_id(0)
    g = pl.program_id(1)
    half = jnp.left_shift(1, g)
    pos0 = (i * tm) % seq
    u = u_ref[...]
    prev = jnp.where(pos0 > 0, prev_ref[...], 0.0)
    nxt = jnp.where(pos0 + tm < seq, next_ref[...], 0.0)
    ext = jnp.concatenate([prev, u, nxt], axis=0)
    rr = lax.broadcasted_iota(jnp.int32, (tm, tm + 2 * POOL_HALO), 0)
    cc = lax.broadcasted_iota(jnp.int32, (tm, tm + 2 * POOL_HALO), 1) - POOL_HALO
    band = jnp.where((cc >= rr - half) & (cc < rr + half), 1.0, 0.0).astype(BF16)
    hi = ext.astype(BF16)
    lo = (ext - hi.astype(F32)).astype(BF16)
    wsum = (jnp.dot(band, hi, preferred_element_type=F32)
            + jnp.dot(band, lo, preferred_element_type=F32))
    t = pos0 + lax.broadcasted_iota(jnp.int32, (tm, 1), 0)
    cnt = (jnp.minimum(t + half, seq) - jnp.maximum(t - half, 0)).astype(F32)
    pooled = wsum / cnt - u
    y = jnp.dot(pooled.astype(BF16), w_ref[0], preferred_element_type=F32)
    o_ref[...] = (y * scale_ref[...]).astype(o_ref.dtype)


def _pool_mixer(zs, pool_w, pool_scale, *, seq, tm=512):
    m = zs.shape[0]
    ng, c, _ = pool_w.shape
    hb = tm // POOL_HALO
    last = m // POOL_HALO - 1
    return pl.pallas_call(
        functools.partial(_pool_kernel, tm=tm, seq=seq),
        grid=(m // tm, ng),
        in_specs=[pl.BlockSpec((tm, c), lambda i, g: (i, g)),
                  pl.BlockSpec((POOL_HALO, c), lambda i, g: (jnp.maximum(i * hb - 1, 0), g)),
                  pl.BlockSpec((POOL_HALO, c), lambda i, g: (jnp.minimum((i + 1) * hb, last), g)),
                  pl.BlockSpec((1, c, c), lambda i, g: (g, 0, 0)),
                  pl.BlockSpec((1, c), lambda i, g: (0, g))],
        out_specs=pl.BlockSpec((tm, c), lambda i, g: (i, g)),
        out_shape=jax.ShapeDtypeStruct((m, ng * c), BF16),
        compiler_params=_params("parallel", "arbitrary"),
        name="pool_mixer",
    )(zs, zs, zs, pool_w, pool_scale.reshape(1, ng * c))


def _fold_rope(t):
    folded = t + pltpu.roll(t, ROPE_DIM, axis=1)
    lane = lax.broadcasted_iota(jnp.int32, t.shape, 1)
    return jnp.where(lane < ROPE_DIM, folded, 0.0)


def _qproj_kernel(c_ref, ga_ref, w_ref, gn_ref, gr_ref, tab_ref, o_ref, *, out_scale):
    x = c_ref[...]
    ms = jnp.mean(x * x, axis=-1, keepdims=True)
    a = (x * lax.rsqrt(ms + RMS_EPS) * ga_ref[...]).astype(BF16)
    gn = gn_ref[...] * out_scale
    rope_mul = gr_ref[...] * tab_ref[...] * out_scale
    for h in range(w_ref.shape[0]):
        y = jnp.dot(a, w_ref[h], preferred_element_type=F32)
        nope = y[:, :NOPE_DIM]
        rr = y[:, NOPE_DIM:]
        ssq = jnp.sum(nope * nope, axis=-1, keepdims=True) + 0.5 * jnp.sum(rr * rr, axis=-1, keepdims=True)
        r = lax.rsqrt(ssq * (1.0 / QK_DIM) + RMS_EPS)
        o_ref[0, h, :, :NOPE_DIM] = (nope * r * gn).astype(o_ref.dtype)
        o_ref[0, h, :, NOPE_DIM:] = _fold_rope(rr * r * rope_mul).astype(o_ref.dtype)


def _q_proj(zs, col_block, ga, wq, gn, gr, tab, *, n_seq, seq, out_scale, tm=512):
    m = zs.shape[0]
    nh, kq, _ = wq.shape
    spt = seq // tm
    return pl.pallas_call(
        functools.partial(_qproj_kernel, out_scale=out_scale),
        grid=(m // tm,),
        in_specs=[pl.BlockSpec((tm, kq), lambda i: (i, col_block)),
                  pl.BlockSpec((1, kq), lambda i: (0, 0)),
                  pl.BlockSpec((nh, kq, HEAD_PAD), lambda i: (0, 0, 0)),
                  pl.BlockSpec((1, NOPE_DIM), lambda i: (0, 0)),
                  pl.BlockSpec((1, 2 * ROPE_DIM), lambda i: (0, 0)),
                  pl.BlockSpec((tm, 2 * ROPE_DIM), lambda i: (i % spt, 0))],
        out_specs=pl.BlockSpec((1, nh, tm, HEAD_PAD), lambda i: (i // spt, 0, i % spt, 0)),
        out_shape=jax.ShapeDtypeStruct((n_seq, nh, seq, HEAD_PAD), BF16),
        compiler_params=_params("parallel"),
        name="q_proj",
    )(zs, ga, wq, gn, gr, tab)


def _kvproj_kernel(c_ref, kr_ref, ga_ref, wk_ref, wvt_ref, gn_ref, gr_ref, tab_ref, k_ref, vt_ref):
    x = c_ref[...]
    ms = jnp.mean(x * x, axis=-1, keepdims=True)
    a = (x * lax.rsqrt(ms + RMS_EPS) * ga_ref[...]).astype(BF16)
    kr = kr_ref[...]
    ssq_rope = 0.5 * jnp.sum(kr * kr, axis=-1, keepdims=True)
    rope = _fold_rope(kr * gr_ref[...] * tab_ref[...])
    gn = gn_ref[...]
    pad_shape = (vt_ref.shape[3] - V_DIM, vt_ref.shape[4])
    ones_rows = jnp.where(lax.broadcasted_iota(jnp.int32, pad_shape, 0) == 0, 1.0, 0.0).astype(vt_ref.dtype)
    for h in range(wk_ref.shape[0]):
        nope = jnp.dot(a, wk_ref[h], preferred_element_type=F32)
        ssq = jnp.sum(nope * nope, axis=-1, keepdims=True) + ssq_rope
        r = lax.rsqrt(ssq * (1.0 / QK_DIM) + RMS_EPS)
        k_ref[0, h, :, :NOPE_DIM] = (nope * r * gn).astype(k_ref.dtype)
        k_ref[0, h, :, NOPE_DIM:] = (rope * r).astype(k_ref.dtype)
        vt = lax.dot_general(wvt_ref[h], a, (((1,), (1,)), ((), ())), preferred_element_type=F32)
        vt_ref[0, h, 0, :V_DIM] = vt.astype(vt_ref.dtype)
        vt_ref[0, h, 0, V_DIM:] = ones_rows


def _kv_proj(zs, ckv_block, kr_block, ga, wk, wvt, gn, gr, tab, *, n_seq, seq, tm):
    m = zs.shape[0]
    nh, kc, _ = wk.shape
    spt = seq // tm
    return pl.pallas_call(
        _kvproj_kernel,
        grid=(m // tm,),
        in_specs=[pl.BlockSpec((tm, kc), lambda i: (i, ckv_block)),
                  pl.BlockSpec((tm, 2 * ROPE_DIM), lambda i: (i, kr_block)),
                  pl.BlockSpec((1, kc), lambda i: (0, 0)),
                  pl.BlockSpec((nh, kc, NOPE_DIM), lambda i: (0, 0, 0)),
                  pl.BlockSpec((nh, V_DIM, kc), lambda i: (0, 0, 0)),
                  pl.BlockSpec((1, NOPE_DIM), lambda i: (0, 0)),
                  pl.BlockSpec((1, 2 * ROPE_DIM), lambda i: (0, 0)),
                  pl.BlockSpec((tm, 2 * ROPE_DIM), lambda i: (i % spt, 0))],
        out_specs=[pl.BlockSpec((1, nh, tm, HEAD_PAD), lambda i: (i // spt, 0, i % spt, 0)),
                   pl.BlockSpec((1, nh, 1, V_ROWS, tm), lambda i: (i // spt, 0, i % spt, 0, 0))],
        out_shape=[jax.ShapeDtypeStruct((n_seq, nh, seq, HEAD_PAD), BF16),
                   jax.ShapeDtypeStruct((n_seq, nh, spt, V_ROWS, tm), BF16)],
        compiler_params=_params("parallel"),
        name="kv_proj",
    )(zs, zs, ga, wk, wvt, gn, gr, tab)


def _attn_kernel(q_ref, k_ref, vt_ref, o_ref, s_even, s_odd, acc_scr):
    tk, tq = s_even.shape
    nk = vt_ref.shape[2]
    nq = q_ref.shape[2] // tq
    assert nk % 2 == 0 and nk >= 2

    def q_tile(i):
        return q_ref[0, 0, pl.ds(pl.multiple_of(i * tq, tq), tq), :]

    def scores(q, j, dst):
        k = k_ref[0, 0, pl.ds(pl.multiple_of(j * tk, tk), tk), :]
        dst[...] = lax.dot_general(k, q, (((1,), (1,)), ((), ())), preferred_element_type=F32)

    def update(j, src, m):
        st = src[...]
        m_new = jnp.maximum(m, jnp.max(st, axis=0, keepdims=True))
        alpha = jnp.exp2(m - m_new)
        p = jnp.exp2(st - m_new).astype(BF16)
        pv = jnp.dot(vt_ref[0, 0, j], p, preferred_element_type=F32)
        acc_scr[...] = alpha * acc_scr[...] + pv
        return m_new

    scores(q_tile(0), 0, s_even)

    unroll = min(ATTN_UNROLL, nk)
    assert unroll % 2 == 0 and nk % unroll == 0
    bufs = (s_even, s_odd)

    def per_q_tile(i, _):
        q = q_tile(i)
        acc_scr[...] = jnp.zeros_like(acc_scr)

        def chunk_group(j0, m, q_after):
            for u in range(unroll):
                if u + 1 < unroll:
                    scores(q, j0 + u + 1, bufs[(u + 1) % 2])
                else:
                    q_next, j_next = q_after
                    scores(q_next, j_next, bufs[0])
                m = update(j0 + u, bufs[u % 2], m)
            return m

        m = lax.fori_loop(
            0, nk // unroll - 1,
            lambda g, c: chunk_group(g * unroll, c, (q, (g + 1) * unroll)),
            jnp.full((1, tq), -jnp.inf, F32))
        chunk_group(nk - unroll, m, (q_tile(jnp.minimum(i + 1, nq - 1)), 0))
        out = acc_scr[:V_DIM] / acc_scr[V_DIM:V_DIM + 1]
        o_ref[0, pl.ds(pl.multiple_of(i * tq, tq), tq), :] = out.T.astype(o_ref.dtype)
        return 0

    lax.fori_loop(0, nq, per_q_tile, 0)


def _attention(q, k, vt, *, tq=1024):
    n_seq, nh, seq, _ = q.shape
    nk, v_rows, tk = vt.shape[2:]
    return pl.pallas_call(
        _attn_kernel,
        grid=(n_seq, nh),
        in_specs=[pl.BlockSpec((1, 1, seq, HEAD_PAD), lambda b, h: (b, h, 0, 0)),
                  pl.BlockSpec((1, 1, seq, HEAD_PAD), lambda b, h: (b, h, 0, 0)),
                  pl.BlockSpec((1, 1, nk, v_rows, tk), lambda b, h: (b, h, 0, 0, 0))],
        out_specs=pl.BlockSpec((1, seq, V_DIM), lambda b, h: (b, 0, h)),
        out_shape=jax.ShapeDtypeStruct((n_seq, seq, nh * V_DIM), BF16),
        scratch_shapes=[pltpu.VMEM((tk, tq), F32), pltpu.VMEM((tk, tq), F32), pltpu.VMEM((v_rows, tq), F32)],
        compiler_params=_params("parallel", "arbitrary"),
        name="attention",
    )(q, k, vt)


def _merge_kernel(a_ref, b_ref, wa_ref, wb_ref, ga_ref, gb_ref, o_ref):
    pa = jnp.dot(a_ref[...], wa_ref[...], preferred_element_type=F32)
    pb = jnp.dot(b_ref[...], wb_ref[...], preferred_element_type=F32)
    o_ref[...] = (ga_ref[...].astype(F32) * pa + gb_ref[...].astype(F32) * pb).astype(o_ref.dtype)


def _gated_merge(a, b, wa, wb, gates, *, tm=1024, tn=1024):
    m, k = a.shape
    n = wa.shape[1]
    nj = n // tn
    return pl.pallas_call(
        _merge_kernel,
        grid=(m // tm, nj),
        in_specs=[pl.BlockSpec((tm, k), lambda i, j: (i, 0)),
                  pl.BlockSpec((tm, k), lambda i, j: (i, 0)),
                  pl.BlockSpec((k, tn), lambda i, j: (0, j)),
                  pl.BlockSpec((k, tn), lambda i, j: (0, j)),
                  pl.BlockSpec((tm, tn), lambda i, j: (i, j)),
                  pl.BlockSpec((tm, tn), lambda i, j: (i, j + nj))],
        out_specs=pl.BlockSpec((tm, tn), lambda i, j: (i, j)),
        out_shape=jax.ShapeDtypeStruct((m, n), BF16),
        compiler_params=_params("parallel", "arbitrary"),
        name="gated_merge",
    )(a, b, wa, wb, gates, gates)


def _ffn_up_kernel(x_ref, prev_ref, next_ref, gain_ref, wg_ref, wv_ref, cwg_ref, cwv_ref, cbg_ref, cbv_ref,
                   o_ref, a_scr, *, tm, seq):
    @pl.when(pl.program_id(1) == 0)
    def _():
        def normed(x, keep=None):
            ms = jnp.mean(x * x, axis=-1, keepdims=True)
            h = x * lax.rsqrt(ms + RMS_EPS) * gain_ref[...]
            return (h if keep is None else jnp.where(keep, h, 0.0)).astype(a_scr.dtype)

        pos0 = (pl.program_id(0) * tm) % seq
        a_scr[:CONV_HALO] = normed(prev_ref[...], pos0 > 0)
        a_scr[CONV_HALO + tm:] = normed(next_ref[...], pos0 + tm < seq)
        for r in range(0, tm, FFN_NORM_ROWS):
            a_scr[CONV_HALO + r:CONV_HALO + r + FFN_NORM_ROWS] = normed(x_ref[r:r + FFN_NORM_ROWS])

    ext = tm + 2 * CONV_HALO
    chunk = ext // FFN_ROW_CHUNKS
    assert chunk * FFN_ROW_CHUNKS == ext and chunk % 16 == 0

    def conv(u, centre, n, cw_ref, cb_ref):
        cw = cw_ref[...]
        rows = u.shape[0]
        before = pltpu.roll(u, 1, 0)[centre:centre + n]
        after = pltpu.roll(u, rows - 1, 0)[centre:centre + n]
        return before * cw[0:1] + u[centre:centre + n] * cw[1:2] + after * cw[2:3] + cb_ref[...]

    ug = uv = None
    have_lo = 0
    done = 0
    for c in range(FFN_ROW_CHUNKS):
        a = a_scr[c * chunk:(c + 1) * chunk]
        ug_c = jnp.dot(a, wg_ref[...], preferred_element_type=F32)
        uv_c = jnp.dot(a, wv_ref[...], preferred_element_type=F32)
        ug = ug_c if ug is None else jnp.concatenate([ug, ug_c], axis=0)
        uv = uv_c if uv is None else jnp.concatenate([uv, uv_c], axis=0)
        upto = tm if c == FFN_ROW_CHUNKS - 1 else ((c + 1) * chunk - CONV_HALO - 1) // 16 * 16
        n = upto - done
        centre = done + CONV_HALO - have_lo
        gate = conv(ug, centre, n, cwg_ref, cbg_ref)
        val = conv(uv, centre, n, cwv_ref, cbv_ref)
        half_gate = 0.5 * gate
        o_ref[done:upto] = ((half_gate * jnp.tanh(half_gate) + half_gate) * val).astype(o_ref.dtype)
        keep_lo = (upto + CONV_HALO - 1) // 8 * 8
        ug, uv = ug[keep_lo - have_lo:], uv[keep_lo - have_lo:]
        have_lo, done = keep_lo, upto


def _ffn_up(x, gain, w_up, conv_w, conv_b, *, seq, tm=1024, tn=256):
    m, d = x.shape
    dff = w_up.shape[1] // 2
    nj = dff // tn
    hb = tm // CONV_HALO
    last = m // CONV_HALO - 1
    return pl.pallas_call(
        functools.partial(_ffn_up_kernel, tm=tm, seq=seq),
        grid=(m // tm, nj),
        in_specs=[pl.BlockSpec((tm, d), lambda i, j: (i, 0)),
                  pl.BlockSpec((CONV_HALO, d), lambda i, j: (jnp.maximum(i * hb - 1, 0), 0)),
                  pl.BlockSpec((CONV_HALO, d), lambda i, j: (jnp.minimum((i + 1) * hb, last), 0)),
                  pl.BlockSpec((1, d), lambda i, j: (0, 0)),
                  pl.BlockSpec((d, tn), lambda i, j: (0, j)),
                  pl.BlockSpec((d, tn), lambda i, j: (0, j + nj)),
                  pl.BlockSpec((3, tn), lambda i, j: (0, j)),
                  pl.BlockSpec((3, tn), lambda i, j: (0, j + nj)),
                  pl.BlockSpec((1, tn), lambda i, j: (0, j)),
                  pl.BlockSpec((1, tn), lambda i, j: (0, j + nj))],
        out_specs=pl.BlockSpec((tm, tn), lambda i, j: (i, j)),
        out_shape=jax.ShapeDtypeStruct((m, dff), BF16),
        scratch_shapes=[pltpu.VMEM((tm + 2 * CONV_HALO, d), BF16)],
        compiler_params=_params("parallel", "arbitrary"),
        name="ffn_up",
    )(x, x, x, gain.reshape(1, d), w_up, w_up, conv_w, conv_w, conv_b, conv_b)


def _swap_halves(a):
    h = a.shape[-1] // 2
    return jnp.concatenate([a[..., h:], a[..., :h]], axis=-1)


def _layer(x_parts, n_seq, seq, norm_mix_gain, w_in, pool_w, pool_scale, q_a_norm_gain, w_uq,
           kv_a_norm_gain, w_ukv, q_norm_gain, k_norm_gain, w_branch_pool, w_branch_mla, w_o,
           norm_ffn_gain, w_up, conv_w, conv_b, w_down):
    d = x_parts[0].shape[1]
    ng, pc, _ = pool_w.shape
    pool_width = ng * pc
    q_lora = q_a_norm_gain.shape[0]
    kv_lora = kv_a_norm_gain.shape[0]
    off_cq = pool_width
    off_ckv = off_cq + q_lora
    off_kr = off_ckv + kv_lora
    off_gp = off_kr + ROPE_DIM

    w_kr = w_in[:, off_kr:off_gp]
    small_cols = off_kr + 2 * ROPE_DIM
    small_pad = (-small_cols) % 768
    w_small = jnp.concatenate(
        [w_in[:, :off_kr], w_kr, _swap_halves(w_kr), jnp.zeros((d, small_pad), F32)], axis=1).astype(BF16)
    w_gate = w_in[:, off_gp:].astype(BF16)

    wq = w_uq.reshape(q_lora, N_HEADS, QK_DIM)
    wq = jnp.concatenate([wq, _swap_halves(wq[..., NOPE_DIM:])], axis=-1)
    wq = wq.transpose(1, 0, 2).astype(BF16)
    wkv = w_ukv.reshape(kv_lora, N_HEADS, NOPE_DIM + V_DIM).astype(BF16)
    wk = wkv[..., :NOPE_DIM].transpose(1, 0, 2)
    wvt = wkv[..., NOPE_DIM:].transpose(1, 2, 0)

    def gains(g):
        gr = g[NOPE_DIM:]
        return g[:NOPE_DIM].reshape(1, NOPE_DIM), jnp.concatenate([gr, _swap_halves(gr)]).reshape(1, 2 * ROPE_DIM)

    qgn, qgr = gains(q_norm_gain)
    kgn, kgr = gains(k_norm_gain)

    inv = 1.0 / (ROPE_THETA ** (jnp.arange(0, ROPE_DIM, 2, dtype=F32) / ROPE_DIM))
    ang = jnp.arange(seq, dtype=F32)[:, None] * inv[None, :]
    cos, sin = jnp.cos(ang), jnp.sin(ang)
    rope_tab = jnp.concatenate([cos, cos, -sin, sin], axis=1)

    h = _rmsnorm(x_parts, norm_mix_gain)
    zs = _matmul(h, w_small, tm=1024, tn=768, out_dtype=F32, name="in_proj_small")
    gates = _matmul(h, w_gate, tm=1024, tn=1024, out_dtype=BF16, gate=True, name="in_proj_gates")

    a_out = _pool_mixer(zs, pool_w.astype(BF16), pool_scale, seq=seq)

    q_scale = math.log2(math.e) / math.sqrt(QK_DIM)
    q = _q_proj(zs, off_cq // q_lora, q_a_norm_gain.reshape(1, q_lora), wq, qgn, qgr, rope_tab,
                n_seq=n_seq, seq=seq, out_scale=q_scale)
    k, vt = _kv_proj(zs, off_ckv // kv_lora, off_kr // (2 * ROPE_DIM), kv_a_norm_gain.reshape(1, kv_lora),
                     wk, wvt, kgn, kgr, rope_tab, n_seq=n_seq, seq=seq, tm=ATTN_KEY_TILE)
    b_out = _attention(q, k, vt).reshape(n_seq * seq, N_HEADS * V_DIM)

    merged = _gated_merge(a_out, b_out, w_branch_pool.astype(BF16), w_branch_mla.astype(BF16), gates)
    x1 = _matmul_residual(merged, w_o.astype(BF16), x_parts, tm=1024, tn=512, name="out_proj")

    act = _ffn_up(x1, norm_ffn_gain, w_up.astype(BF16), conv_w, conv_b.reshape(1, -1), seq=seq)
    return _matmul_residual(act, w_down.astype(BF16), (x1,), out_rows=tuple(p.shape[0] for p in x_parts),
                            tm=512, tn=512, name="down_proj")


def kernel(x_prompt, x_sample, norm_mix_gain, w_in, pool_w, pool_scale, q_a_norm_gain, w_uq,
           kv_a_norm_gain, w_ukv, q_norm_gain, k_norm_gain, w_branch_pool, w_branch_mla, w_o,
           norm_ffn_gain, w_up, conv_w, conv_b, w_down):
    bp, seq, d = x_prompt.shape
    bs = x_sample.shape[0]
    assert x_sample.shape[1:] == (seq, d)
    n_seq = bp + bs
    y = (x_prompt.reshape(bp * seq, d), x_sample.reshape(bs * seq, d))
    for l in range(norm_mix_gain.shape[0]):
        y = _layer(y, n_seq, seq, norm_mix_gain[l], w_in[l], pool_w[l], pool_scale[l], q_a_norm_gain[l],
                   w_uq[l], kv_a_norm_gain[l], w_ukv[l], q_norm_gain[l], k_norm_gain[l],
                   w_branch_pool[l], w_branch_mla[l], w_o[l], norm_ffn_gain[l], w_up[l],
                   conv_w[l], conv_b[l], w_down[l])
    return (y[0].reshape(bp, seq, d), y[1].reshape(bs, seq, d))
```

```python
import functools
import math

import jax
import jax.numpy as jnp
from jax import lax
from jax.experimental import pallas as pl
from jax.experimental.pallas import tpu as pltpu

F32 = jnp.float32
BF16 = jnp.bfloat16

RMS_EPS = 1e-6
ROPE_THETA = 10000.0
N_HEADS = 16
NOPE_DIM = 128
ROPE_DIM = 64
V_DIM = 128
QK_DIM = NOPE_DIM + ROPE_DIM
HEAD_PAD = 256
V_ROWS = V_DIM + 16
POOL_WINDOWS = (2, 4, 8, 16)
POOL_HALO = 8
CONV_HALO = 16
IN_PROJ_SMALL_TN = 1280
FFN_NORM_ROWS = 128
FFN_ROW_CHUNKS = 2
ATTN_UNROLL = 16
ATTN_KEY_TILE = 1024

V7X_VMEM_LIMIT = 56 * 1024 * 1024


def _params(*sem):
    return pltpu.CompilerParams(dimension_semantics=sem, vmem_limit_bytes=V7X_VMEM_LIMIT)


def _stack_tiles(parts, tm):
    counts = [p.shape[0] // tm for p in parts]
    offsets = [sum(counts[:k]) for k in range(len(parts))]
    return counts, offsets


def _stack_spec(block, count, offset, col=None, ncols=1):
    def index(i, *j):
        row = jnp.clip(i - offset, 0, count - 1)
        if col is None:
            return row, 0
        active = (i >= offset) & (i < offset + count)
        frozen = jnp.where(i < offset, 0, ncols - 1)
        return row, jnp.where(active, col(*j), frozen)
    return pl.BlockSpec(block, index)


def _stack_read(refs, counts, offsets, i):
    x = refs[-1][...]
    for k in range(len(refs) - 2, -1, -1):
        x = jnp.where(i < offsets[k] + counts[k], refs[k][...], x)
    return x


def _rmsnorm_kernel(*refs, counts, offsets):
    *x_refs, g_ref, o_ref = refs
    x = _stack_read(x_refs, counts, offsets, pl.program_id(0))
    ms = jnp.mean(x * x, axis=-1, keepdims=True)
    o_ref[...] = (x * lax.rsqrt(ms + RMS_EPS) * g_ref[...]).astype(o_ref.dtype)


def _rmsnorm(parts, gain, tr=256):
    d = parts[0].shape[1]
    counts, offsets = _stack_tiles(parts, tr)
    return pl.pallas_call(
        functools.partial(_rmsnorm_kernel, counts=counts, offsets=offsets),
        grid=(sum(counts),),
        in_specs=[_stack_spec((tr, d), c, o) for c, o in zip(counts, offsets)]
                 + [pl.BlockSpec((1, d), lambda i: (0, 0))],
        out_specs=pl.BlockSpec((tr, d), lambda i: (i, 0)),
        out_shape=jax.ShapeDtypeStruct((sum(counts) * tr, d), BF16),
        compiler_params=_params("arbitrary"),
        name="rmsnorm",
    )(*parts, gain.reshape(1, d))


def _sigmoid(x):
    return 0.5 * jnp.tanh(0.5 * x) + 0.5


def _mm_kernel(a_ref, b_ref, o_ref, *, gate):
    acc = jnp.dot(a_ref[...], b_ref[...], preferred_element_type=F32)
    if gate:
        acc = _sigmoid(acc)
    o_ref[...] = acc.astype(o_ref.dtype)


def _matmul(a, b, *, tm, tn, out_dtype, gate=False, name):
    m, k = a.shape
    _, n = b.shape
    return pl.pallas_call(
        functools.partial(_mm_kernel, gate=gate),
        grid=(m // tm, n // tn),
        in_specs=[pl.BlockSpec((tm, k), lambda i, j: (i, 0)),
                  pl.BlockSpec((k, tn), lambda i, j: (0, j))],
        out_specs=pl.BlockSpec((tm, tn), lambda i, j: (i, j)),
        out_shape=jax.ShapeDtypeStruct((m, n), out_dtype),
        compiler_params=_params("parallel", "arbitrary"),
        name=name,
    )(a, b)


def _mm_res_kernel(*refs, n_res, res_tiles, out_tiles):
    a_ref, b_ref = refs[:2]
    r_refs, o_refs = refs[2:2 + n_res], refs[2 + n_res:]
    i = pl.program_id(0)
    acc = jnp.dot(a_ref[...], b_ref[...], preferred_element_type=F32)
    y = _stack_read(r_refs, *res_tiles, i) + acc
    if len(o_refs) == 1:
        o_refs[0][...] = y
    else:
        for o_ref, count, offset in zip(o_refs, *out_tiles):
            @pl.when((i >= offset) & (i < offset + count))
            def _(o_ref=o_ref):
                o_ref[...] = y


def _matmul_residual(a, b, res_parts, *, out_rows=None, tm, tn, name):
    m, k = a.shape
    _, n = b.shape
    nj = n // tn
    res_tiles = _stack_tiles(res_parts, tm)
    out_rows = out_rows or (m,)
    out_tiles = [r // tm for r in out_rows], [sum(out_rows[:q]) // tm for q in range(len(out_rows))]
    col = lambda j: j
    if len(out_rows) == 1:
        out_specs = pl.BlockSpec((tm, tn), lambda i, j: (i, j))
        out_shape = jax.ShapeDtypeStruct((m, n), F32)
    else:
        out_specs = [_stack_spec((tm, tn), c, o, col, nj) for c, o in zip(*out_tiles)]
        out_shape = [jax.ShapeDtypeStruct((r, n), F32) for r in out_rows]
    return pl.pallas_call(
        functools.partial(_mm_res_kernel, n_res=len(res_parts), res_tiles=res_tiles, out_tiles=out_tiles),
        grid=(m // tm, nj),
        in_specs=[pl.BlockSpec((tm, k), lambda i, j: (i, 0)),
                  pl.BlockSpec((k, tn), lambda i, j: (0, j))]
                 + [_stack_spec((tm, tn), c, o, col, nj) for c, o in zip(*res_tiles)],
        out_specs=out_specs,
        out_shape=out_shape,
        compiler_params=_params("arbitrary", "arbitrary"),
        name=name,
    )(a, b, *res_parts)


def _pool_kernel(u_ref, prev_ref, next_ref, w_ref, scale_ref, o_ref, *, tm, seq):
    i = pl.program_id(0)
    g = pl.program_id(1)
    half = jnp.left_shift(1, g)
    pos0 = (i * tm) % seq
    u = u_ref[...]
    prev = jnp.where(pos0 > 0, prev_ref[...], 0.0)
    nxt = jnp.where(pos0 + tm < seq, next_ref[...], 0.0)
    ext = jnp.concatenate([prev, u, nxt], axis=0)
    rr = lax.broadcasted_iota(jnp.int32, (tm, tm + 2 * POOL_HALO), 0)
    cc = lax.broadcasted_iota(jnp.int32, (tm, tm + 2 * POOL_HALO), 1) - POOL_HALO
    band = jnp.where((cc >= rr - half) & (cc < rr + half), 1.0, 0.0).astype(BF16)
    hi = ext.astype(BF16)
    lo = (ext - hi.astype(F32)).astype(BF16)
    wsum = (jnp.dot(band, hi, preferred_element_type=F32)
            + jnp.dot(band, lo, preferred_element_type=F32))
    t = pos0 + lax.broadcasted_iota(jnp.int32, (tm, 1), 0)
    cnt = (jnp.minimum(t + half, seq) - jnp.maximum(t - half, 0)).astype(F32)
    pooled = wsum / cnt - u
    y = jnp.dot(pooled.astype(BF16), w_ref[0], preferred_element_type=F32)
    o_ref[...] = (y * scale_ref[...]).astype(o_ref.dtype)


def _pool_mixer(zs, pool_w, pool_scale, *, seq, tm=512):
    m = zs.shape[0]
    ng, c, _ = pool_w.shape
    hb = tm // POOL_HALO
    last = m // POOL_HALO - 1
    return pl.pallas_call(
        functools.partial(_pool_kernel, tm=tm, seq=seq),
        grid=(m // tm, ng),
        in_specs=[pl.BlockSpec((tm, c), lambda i, g: (i, g)),
                  pl.BlockSpec((POOL_HALO, c), lambda i, g: (jnp.maximum(i * hb - 1, 0), g)),
                  pl.BlockSpec((POOL_HALO, c), lambda i, g: (jnp.minimum((i + 1) * hb, last), g)),
                  pl.BlockSpec((1, c, c), lambda i, g: (g, 0, 0)),
                  pl.BlockSpec((1, c), lambda i, g: (0, g))],
        out_specs=pl.BlockSpec((tm, c), lambda i, g: (i, g)),
        out_shape=jax.ShapeDtypeStruct((m, ng * c), BF16),
        compiler_params=_params("parallel", "arbitrary"),
        name="pool_mixer",
    )(zs, zs, zs, pool_w, pool_scale.reshape(1, ng * c))


def _fold_rope(t):
    folded = t + pltpu.roll(t, ROPE_DIM, axis=1)
    lane = lax.broadcasted_iota(jnp.int32, t.shape, 1)
    return jnp.where(lane < ROPE_DIM, folded, 0.0)


def _qproj_kernel(c_ref, ga_ref, w_ref, gn_ref, gr_ref, tab_ref, o_ref, *, out_scale):
    x = c_ref[...]
    ms = jnp.mean(x * x, axis=-1, keepdims=True)
    a = (x * lax.rsqrt(ms + RMS_EPS) * ga_ref[...]).astype(BF16)
    gn = gn_ref[...] * out_scale
    rope_mul = gr_ref[...] * tab_ref[...] * out_scale
    for h in range(w_ref.shape[0]):
        y = jnp.dot(a, w_ref[h], preferred_element_type=F32)
        nope = y[:, :NOPE_DIM]
        rr = y[:, NOPE_DIM:]
        ssq = jnp.sum(nope * nope, axis=-1, keepdims=True) + 0.5 * jnp.sum(rr * rr, axis=-1, keepdims=True)
        r = lax.rsqrt(ssq * (1.0 / QK_DIM) + RMS_EPS)
        o_ref[0, h, :, :NOPE_DIM] = (nope * r * gn).astype(o_ref.dtype)
        o_ref[0, h, :, NOPE_DIM:] = _fold_rope(rr * r * rope_mul).astype(o_ref.dtype)


def _q_proj(zs, col_block, ga, wq, gn, gr, tab, *, n_seq, seq, out_scale, tm=512):
    m = zs.shape[0]
    nh, kq, _ = wq.shape
    spt = seq // tm
    return pl.pallas_call(
        functools.partial(_qproj_kernel, out_scale=out_scale),
        grid=(m // tm,),
        in_specs=[pl.BlockSpec((tm, kq), lambda i: (i, col_block)),
                  pl.BlockSpec((1, kq), lambda i: (0, 0)),
                  pl.BlockSpec((nh, kq, HEAD_PAD), lambda i: (0, 0, 0)),
                  pl.BlockSpec((1, NOPE_DIM), lambda i: (0, 0)),
                  pl.BlockSpec((1, 2 * ROPE_DIM), lambda i: (0, 0)),
                  pl.BlockSpec((tm, 2 * ROPE_DIM), lambda i: (i % spt, 0))],
        out_specs=pl.BlockSpec((1, nh, tm, HEAD_PAD), lambda i: (i // spt, 0, i % spt, 0)),
        out_shape=jax.ShapeDtypeStruct((n_seq, nh, seq, HEAD_PAD), BF16),
        compiler_params=_params("parallel"),
        name="q_proj",
    )(zs, ga, wq, gn, gr, tab)


def _kvproj_kernel(c_ref, kr_ref, ga_ref, wk_ref, wvt_ref, gn_ref, gr_ref, tab_ref, k_ref, vt_ref):
    x = c_ref[...]
    ms = jnp.mean(x * x, axis=-1, keepdims=True)
    a = (x * lax.rsqrt(ms + RMS_EPS) * ga_ref[...]).astype(BF16)
    kr = kr_ref[...]
    ssq_rope = 0.5 * jnp.sum(kr * kr, axis=-1, keepdims=True)
    rope = _fold_rope(kr * gr_ref[...] * tab_ref[...])
    gn = gn_ref[...]
    pad_shape = (vt_ref.shape[3] - V_DIM, vt_ref.shape[4])
    ones_rows = jnp.where(lax.broadcasted_iota(jnp.int32, pad_shape, 0) == 0, 1.0, 0.0).astype(vt_ref.dtype)
    for h in range(wk_ref.shape[0]):
        nope = jnp.dot(a, wk_ref[h], preferred_element_type=F32)
        ssq = jnp.sum(nope * nope, axis=-1, keepdims=True) + ssq_rope
        r = lax.rsqrt(ssq * (1.0 / QK_DIM) + RMS_EPS)
        k_ref[0, h, :, :NOPE_DIM] = (nope * r * gn).astype(k_ref.dtype)
        k_ref[0, h, :, NOPE_DIM:] = (rope * r).astype(k_ref.dtype)
        vt = lax.dot_general(wvt_ref[h], a, (((1,), (1,)), ((), ())), preferred_element_type=F32)
        vt_ref[0, h, 0, :V_DIM] = vt.astype(vt_ref.dtype)
        vt_ref[0, h, 0, V_DIM:] = ones_rows


def _kv_proj(zs, ckv_block, kr_block, ga, wk, wvt, gn, gr, tab, *, n_seq, seq, tm):
    m = zs.shape[0]
    nh, kc, _ = wk.shape
    spt = seq // tm
    return pl.pallas_call(
        _kvproj_kernel,
        grid=(m // tm,),
        in_specs=[pl.BlockSpec((tm, kc), lambda i: (i, ckv_block)),
                  pl.BlockSpec((tm, 2 * ROPE_DIM), lambda i: (i, kr_block)),
                  pl.BlockSpec((1, kc), lambda i: (0, 0)),
                  pl.BlockSpec((nh, kc, NOPE_DIM), lambda i: (0, 0, 0)),
                  pl.BlockSpec((nh, V_DIM, kc), lambda i: (0, 0, 0)),
                  pl.BlockSpec((1, NOPE_DIM), lambda i: (0, 0)),
                  pl.BlockSpec((1, 2 * ROPE_DIM), lambda i: (0, 0)),
                  pl.BlockSpec((tm, 2 * ROPE_DIM), lambda i: (i % spt, 0))],
        out_specs=[pl.BlockSpec((1, nh, tm, HEAD_PAD), lambda i: (i // spt, 0, i % spt, 0)),
                   pl.BlockSpec((1, nh, 1, V_ROWS, tm), lambda i: (i // spt, 0, i % spt, 0, 0))],
        out_shape=[jax.ShapeDtypeStruct((n_seq, nh, seq, HEAD_PAD), BF16),
                   jax.ShapeDtypeStruct((n_seq, nh, spt, V_ROWS, tm), BF16)],
        compiler_params=_params("parallel"),
        name="kv_proj",
    )(zs, zs, ga, wk, wvt, gn, gr, tab)


def _attn_kernel(q_ref, k_ref, vt_ref, o_ref, s_even, s_odd, acc_scr):
    tk, tq = s_even.shape
    stored = vt_ref.shape[4]
    per_step = tk // stored
    nk = vt_ref.shape[2] // per_step
    nq = q_ref.shape[2] // tq
    assert nk % 2 == 0 and nk >= 2

    def q_tile(i):
        return q_ref[0, 0, pl.ds(pl.multiple_of(i * tq, tq), tq), :]

    def scores(q, j, dst):
        k = k_ref[0, 0, pl.ds(pl.multiple_of(j * tk, tk), tk), :]
        dst[...] = lax.dot_general(k, q, (((1,), (1,)), ((), ())), preferred_element_type=F32)

    def update(j, src, m):
        st = src[...]
        m_new = jnp.maximum(m, jnp.max(st, axis=0, keepdims=True))
        alpha = jnp.exp2(m - m_new)
        p = jnp.exp2(st - m_new).astype(BF16)
        vt = jnp.concatenate([vt_ref[0, 0, j * per_step + c] for c in range(per_step)], axis=1)
        pv = jnp.dot(vt, p, preferred_element_type=F32)
        acc_scr[...] = alpha * acc_scr[...] + pv
        return m_new

    scores(q_tile(0), 0, s_even)

    unroll = min(ATTN_UNROLL, nk)
    assert unroll % 2 == 0 and nk % unroll == 0
    bufs = (s_even, s_odd)

    def per_q_tile(i, _):
        q = q_tile(i)
        acc_scr[...] = jnp.zeros_like(acc_scr)

        def chunk_group(j0, m, q_after):
            for u in range(unroll):
                if u + 1 < unroll:
                    scores(q, j0 + u + 1, bufs[(u + 1) % 2])
                else:
                    q_next, j_next = q_after
                    scores(q_next, j_next, bufs[0])
                m = update(j0 + u, bufs[u % 2], m)
            return m

        m = lax.fori_loop(
            0, nk // unroll - 1,
            lambda g, c: chunk_group(g * unroll, c, (q, (g + 1) * unroll)),
            jnp.full((1, tq), -jnp.inf, F32))
        chunk_group(nk - unroll, m, (q_tile(jnp.minimum(i + 1, nq - 1)), 0))
        out = acc_scr[:V_DIM] / acc_scr[V_DIM:V_DIM + 1]
        o_ref[0, pl.ds(pl.multiple_of(i * tq, tq), tq), :] = out.T.astype(o_ref.dtype)
        return 0

    lax.fori_loop(0, nq, per_q_tile, 0)


def _attention(q, k, vt, *, tq=1024, tk=2048):
    n_seq, nh, seq, _ = q.shape
    nk, v_rows, stored = vt.shape[2:]
    assert tk % stored == 0
    return pl.pallas_call(
        _attn_kernel,
        grid=(n_seq, nh),
        in_specs=[pl.BlockSpec((1, 1, seq, HEAD_PAD), lambda b, h: (b, h, 0, 0)),
                  pl.BlockSpec((1, 1, seq, HEAD_PAD), lambda b, h: (b, h, 0, 0)),
                  pl.BlockSpec((1, 1, nk, v_rows, stored), lambda b, h: (b, h, 0, 0, 0))],
        out_specs=pl.BlockSpec((1, seq, V_DIM), lambda b, h: (b, 0, h)),
        out_shape=jax.ShapeDtypeStruct((n_seq, seq, nh * V_DIM), BF16),
        scratch_shapes=[pltpu.VMEM((tk, tq), F32), pltpu.VMEM((tk, tq), F32), pltpu.VMEM((v_rows, tq), F32)],
        compiler_params=_params("parallel", "arbitrary"),
        name="attention",
    )(q, k, vt)


def _merge_kernel(a_ref, b_ref, wa_ref, wb_ref, ga_ref, gb_ref, o_ref):
    pa = jnp.dot(a_ref[...], wa_ref[...], preferred_element_type=F32)
    pb = jnp.dot(b_ref[...], wb_ref[...], preferred_element_type=F32)
    o_ref[...] = (ga_ref[...].astype(F32) * pa + gb_ref[...].astype(F32) * pb).astype(o_ref.dtype)


def _gated_merge(a, b, wa, wb, gates, *, tm=1024, tn=1024):
    m, k = a.shape
    n = wa.shape[1]
    nj = n // tn
    return pl.pallas_call(
        _merge_kernel,
        grid=(m // tm, nj),
        in_specs=[pl.BlockSpec((tm, k), lambda i, j: (i, 0)),
                  pl.BlockSpec((tm, k), lambda i, j: (i, 0)),
                  pl.BlockSpec((k, tn), lambda i, j: (0, j)),
                  pl.BlockSpec((k, tn), lambda i, j: (0, j)),
                  pl.BlockSpec((tm, tn), lambda i, j: (i, j)),
                  pl.BlockSpec((tm, tn), lambda i, j: (i, j + nj))],
        out_specs=pl.BlockSpec((tm, tn), lambda i, j: (i, j)),
        out_shape=jax.ShapeDtypeStruct((m, n), BF16),
        compiler_params=_params("parallel", "arbitrary"),
        name="gated_merge",
    )(a, b, wa, wb, gates, gates)


def _ffn_up_kernel(x_ref, prev_ref, next_ref, gain_ref, wg_ref, wv_ref, cwg_ref, cwv_ref, cbg_ref, cbv_ref,
                   o_ref, a_scr, *, tm, seq):
    @pl.when(pl.program_id(1) == 0)
    def _():
        def normed(x, keep=None):
            ms = jnp.mean(x * x, axis=-1, keepdims=True)
            h = x * lax.rsqrt(ms + RMS_EPS) * gain_ref[...]
            return (h if keep is None else jnp.where(keep, h, 0.0)).astype(a_scr.dtype)

        pos0 = (pl.program_id(0) * tm) % seq
        a_scr[:CONV_HALO] = normed(prev_ref[...], pos0 > 0)
        a_scr[CONV_HALO + tm:] = normed(next_ref[...], pos0 + tm < seq)
        for r in range(0, tm, FFN_NORM_ROWS):
            a_scr[CONV_HALO + r:CONV_HALO + r + FFN_NORM_ROWS] = normed(x_ref[r:r + FFN_NORM_ROWS])

    ext = tm + 2 * CONV_HALO
    chunk = ext // FFN_ROW_CHUNKS
    assert chunk * FFN_ROW_CHUNKS == ext and chunk % 16 == 0

    def conv(u, centre, n, cw_ref, cb_ref):
        cw = cw_ref[...]
        rows = u.shape[0]
        before = pltpu.roll(u, 1, 0)[centre:centre + n]
        after = pltpu.roll(u, rows - 1, 0)[centre:centre + n]
        return before * cw[0:1] + u[centre:centre + n] * cw[1:2] + after * cw[2:3] + cb_ref[...]

    ug = uv = None
    have_lo = 0
    done = 0
    for c in range(FFN_ROW_CHUNKS):
        a = a_scr[c * chunk:(c + 1) * chunk]
        ug_c = jnp.dot(a, wg_ref[...], preferred_element_type=F32)
        uv_c = jnp.dot(a, wv_ref[...], preferred_element_type=F32)
        ug = ug_c if ug is None else jnp.concatenate([ug, ug_c], axis=0)
        uv = uv_c if uv is None else jnp.concatenate([uv, uv_c], axis=0)
        upto = tm if c == FFN_ROW_CHUNKS - 1 else ((c + 1) * chunk - CONV_HALO - 1) // 16 * 16
        n = upto - done
        centre = done + CONV_HALO - have_lo
        gate = conv(ug, centre, n, cwg_ref, cbg_ref)
        val = conv(uv, centre, n, cwv_ref, cbv_ref)
        half_gate = 0.5 * gate
        o_ref[done:upto] = ((half_gate * jnp.tanh(half_gate) + half_gate) * val).astype(o_ref.dtype)
        keep_lo = (upto + CONV_HALO - 1) // 8 * 8
        ug, uv = ug[keep_lo - have_lo:], uv[keep_lo - have_lo:]
        have_lo, done = keep_lo, upto


def _ffn_up(x, gain, w_up, conv_w, conv_b, *, seq, tm=1024, tn=256):
    m, d = x.shape
    dff = w_up.shape[1] // 2
    nj = dff // tn
    hb = tm // CONV_HALO
    last = m // CONV_HALO - 1
    return pl.pallas_call(
        functools.partial(_ffn_up_kernel, tm=tm, seq=seq),
        grid=(m // tm, nj),
        in_specs=[pl.BlockSpec((tm, d), lambda i, j: (i, 0)),
                  pl.BlockSpec((CONV_HALO, d), lambda i, j: (jnp.maximum(i * hb - 1, 0), 0)),
                  pl.BlockSpec((CONV_HALO, d), lambda i, j: (jnp.minimum((i + 1) * hb, last), 0)),
                  pl.BlockSpec((1, d), lambda i, j: (0, 0)),
                  pl.BlockSpec((d, tn), lambda i, j: (0, j)),
                  pl.BlockSpec((d, tn), lambda i, j: (0, j + nj)),
                  pl.BlockSpec((3, tn), lambda i, j: (0, j)),
                  pl.BlockSpec((3, tn), lambda i, j: (0, j + nj)),
                  pl.BlockSpec((1, tn), lambda i, j: (0, j)),
                  pl.BlockSpec((1, tn), lambda i, j: (0, j + nj))],
        out_specs=pl.BlockSpec((tm, tn), lambda i, j: (i, j)),
        out_shape=jax.ShapeDtypeStruct((m, dff), BF16),
        scratch_shapes=[pltpu.VMEM((tm + 2 * CONV_HALO, d), BF16)],
        compiler_params=_params("parallel", "arbitrary"),
        name="ffn_up",
    )(x, x, x, gain.reshape(1, d), w_up, w_up, conv_w, conv_w, conv_b, conv_b)


def _swap_halves(a):
    h = a.shape[-1] // 2
    return jnp.concatenate([a[..., h:], a[..., :h]], axis=-1)


def _layer(x_parts, n_seq, seq, norm_mix_gain, w_in, pool_w, pool_scale, q_a_norm_gain, w_uq,
           kv_a_norm_gain, w_ukv, q_norm_gain, k_norm_gain, w_branch_pool, w_branch_mla, w_o,
           norm_ffn_gain, w_up, conv_w, conv_b, w_down):
    d = x_parts[0].shape[1]
    ng, pc, _ = pool_w.shape
    pool_width = ng * pc
    q_lora = q_a_norm_gain.shape[0]
    kv_lora = kv_a_norm_gain.shape[0]
    off_cq = pool_width
    off_ckv = off_cq + q_lora
    off_kr = off_ckv + kv_lora
    off_gp = off_kr + ROPE_DIM

    w_kr = w_in[:, off_kr:off_gp]
    small_cols = off_kr + 2 * ROPE_DIM
    small_pad = (-small_cols) % IN_PROJ_SMALL_TN
    w_small = jnp.concatenate(
        [w_in[:, :off_kr], w_kr, _swap_halves(w_kr), jnp.zeros((d, small_pad), F32)], axis=1).astype(BF16)
    w_gate = w_in[:, off_gp:].astype(BF16)

    wq = w_uq.reshape(q_lora, N_HEADS, QK_DIM)
    wq = jnp.concatenate([wq, _swap_halves(wq[..., NOPE_DIM:])], axis=-1)
    wq = wq.transpose(1, 0, 2).astype(BF16)
    wkv = w_ukv.reshape(kv_lora, N_HEADS, NOPE_DIM + V_DIM).astype(BF16)
    wk = wkv[..., :NOPE_DIM].transpose(1, 0, 2)
    wvt = wkv[..., NOPE_DIM:].transpose(1, 2, 0)

    def gains(g):
        gr = g[NOPE_DIM:]
        return g[:NOPE_DIM].reshape(1, NOPE_DIM), jnp.concatenate([gr, _swap_halves(gr)]).reshape(1, 2 * ROPE_DIM)

    qgn, qgr = gains(q_norm_gain)
    kgn, kgr = gains(k_norm_gain)

    inv = 1.0 / (ROPE_THETA ** (jnp.arange(0, ROPE_DIM, 2, dtype=F32) / ROPE_DIM))
    ang = jnp.arange(seq, dtype=F32)[:, None] * inv[None, :]
    cos, sin = jnp.cos(ang), jnp.sin(ang)
    rope_tab = jnp.concatenate([cos, cos, -sin, sin], axis=1)

    h = _rmsnorm(x_parts, norm_mix_gain)
    zs = _matmul(h, w_small, tm=1024, tn=IN_PROJ_SMALL_TN, out_dtype=F32, name="in_proj_small")
    gates = _matmul(h, w_gate, tm=1024, tn=1024, out_dtype=BF16, gate=True, name="in_proj_gates")

    a_out = _pool_mixer(zs, pool_w.astype(BF16), pool_scale, seq=seq)

    q_scale = math.log2(math.e) / math.sqrt(QK_DIM)
    q = _q_proj(zs, off_cq // q_lora, q_a_norm_gain.reshape(1, q_lora), wq, qgn, qgr, rope_tab,
                n_seq=n_seq, seq=seq, out_scale=q_scale)
    k, vt = _kv_proj(zs, off_ckv // kv_lora, off_kr // (2 * ROPE_DIM), kv_a_norm_gain.reshape(1, kv_lora),
                     wk, wvt, kgn, kgr, rope_tab, n_seq=n_seq, seq=seq, tm=ATTN_KEY_TILE)
    b_out = _attention(q, k, vt).reshape(n_seq * seq, N_HEADS * V_DIM)

    merged = _gated_merge(a_out, b_out, w_branch_pool.astype(BF16), w_branch_mla.astype(BF16), gates)
    x1 = _matmul_residual(merged, w_o.astype(BF16), x_parts, tm=1024, tn=512, name="out_proj")

    act = _ffn_up(x1, norm_ffn_gain, w_up.astype(BF16), conv_w, conv_b.reshape(1, -1), seq=seq)
    return _matmul_residual(act, w_down.astype(BF16), (x1,), out_rows=tuple(p.shape[0] for p in x_parts),
                            tm=512, tn=512, name="down_proj")


def kernel(x_prompt, x_sample, norm_mix_gain, w_in, pool_w, pool_scale, q_a_norm_gain, w_uq,
           kv_a_norm_gain, w_ukv, q_norm_gain, k_norm_gain, w_branch_pool, w_branch_mla, w_o,
           norm_ffn_gain, w_up, conv_w, conv_b, w_down):
    bp, seq, d = x_prompt.shape
    bs = x_sample.shape[0]
    assert x_sample.shape[1:] == (seq, d)
    n_seq = bp + bs
    y = (x_prompt.reshape(bp * seq, d), x_sample.reshape(bs * seq, d))
    for l in range(norm_mix_gain.shape[0]):
        y = _layer(y, n_seq, seq, norm_mix_gain[l], w_in[l], pool_w[l], pool_scale[l], q_a_norm_gain[l],
                   w_uq[l], kv_a_norm_gain[l], w_ukv[l], q_norm_gain[l], k_norm_gain[l],
                   w_branch_pool[l], w_branch_mla[l], w_o[l], norm_ffn_gain[l], w_up[l],
                   conv_w[l], conv_b[l], w_down[l])
    return (y[0].reshape(bp, seq, d), y[1].reshape(bs, seq, d))
```

```python
import functools
import math

import jax
import jax.numpy as jnp
from jax import lax
from jax.experimental import pallas as pl
from jax.experimental.pallas import tpu as pltpu

F32 = jnp.float32
BF16 = jnp.bfloat16

RMS_EPS = 1e-6
ROPE_THETA = 10000.0
N_HEADS = 16
NOPE_DIM = 128
ROPE_DIM = 64
V_DIM = 128
QK_DIM = NOPE_DIM + ROPE_DIM
HEAD_PAD = 256
V_ROWS = V_DIM + 16
POOL_WINDOWS = (2, 4, 8, 16)
POOL_HALO = 8
CONV_HALO = 16
IN_PROJ_SMALL_TN = 1280
FFN_NORM_ROWS = 128
FFN_ROW_CHUNKS = 2
ATTN_UNROLL = 16
ATTN_KEY_TILE = 1024

V7X_VMEM_LIMIT = 56 * 1024 * 1024


def _params(*sem):
    return pltpu.CompilerParams(dimension_semantics=sem, vmem_limit_bytes=V7X_VMEM_LIMIT)


def _stack_tiles(parts, tm):
    counts = [p.shape[0] // tm for p in parts]
    offsets = [sum(counts[:k]) for k in range(len(parts))]
    return counts, offsets


def _stack_spec(block, count, offset, col=None, ncols=1):
    def index(i, *j):
        row = jnp.clip(i - offset, 0, count - 1)
        if col is None:
            return row, 0
        active = (i >= offset) & (i < offset + count)
        frozen = jnp.where(i < offset, 0, ncols - 1)
        return row, jnp.where(active, col(*j), frozen)
    return pl.BlockSpec(block, index)


def _stack_read(refs, counts, offsets, i):
    x = refs[-1][...]
    for k in range(len(refs) - 2, -1, -1):
        x = jnp.where(i < offsets[k] + counts[k], refs[k][...], x)
    return x


def _rmsnorm_kernel(*refs, counts, offsets):
    *x_refs, g_ref, o_ref = refs
    x = _stack_read(x_refs, counts, offsets, pl.program_id(0))
    ms = jnp.mean(x * x, axis=-1, keepdims=True)
    o_ref[...] = (x * lax.rsqrt(ms + RMS_EPS) * g_ref[...]).astype(o_ref.dtype)


def _rmsnorm(parts, gain, tr=256):
    d = parts[0].shape[1]
    counts, offsets = _stack_tiles(parts, tr)
    return pl.pallas_call(
        functools.partial(_rmsnorm_kernel, counts=counts, offsets=offsets),
        grid=(sum(counts),),
        in_specs=[_stack_spec((tr, d), c, o) for c, o in zip(counts, offsets)]
                 + [pl.BlockSpec((1, d), lambda i: (0, 0))],
        out_specs=pl.BlockSpec((tr, d), lambda i: (i, 0)),
        out_shape=jax.ShapeDtypeStruct((sum(counts) * tr, d), BF16),
        compiler_params=_params("arbitrary"),
        name="rmsnorm",
    )(*parts, gain.reshape(1, d))


def _sigmoid(x):
    return 0.5 * jnp.tanh(0.5 * x) + 0.5


def _mm_kernel(a_ref, b_ref, o_ref, *, gate):
    acc = jnp.dot(a_ref[...], b_ref[...], preferred_element_type=F32)
    if gate:
        acc = _sigmoid(acc)
    o_ref[...] = acc.astype(o_ref.dtype)


def _matmul(a, b, *, tm, tn, out_dtype, gate=False, name):
    m, k = a.shape
    _, n = b.shape
    return pl.pallas_call(
        functools.partial(_mm_kernel, gate=gate),
        grid=(m // tm, n // tn),
        in_specs=[pl.BlockSpec((tm, k), lambda i, j: (i, 0)),
                  pl.BlockSpec((k, tn), lambda i, j: (0, j))],
        out_specs=pl.BlockSpec((tm, tn), lambda i, j: (i, j)),
        out_shape=jax.ShapeDtypeStruct((m, n), out_dtype),
        compiler_params=_params("parallel", "arbitrary"),
        name=name,
    )(a, b)


def _mm_res_kernel(*refs, n_res, res_tiles, out_tiles):
    a_ref, b_ref = refs[:2]
    r_refs, o_refs = refs[2:2 + n_res], refs[2 + n_res:]
    i = pl.program_id(0)
    acc = jnp.dot(a_ref[...], b_ref[...], preferred_element_type=F32)
    y = _stack_read(r_refs, *res_tiles, i) + acc
    if len(o_refs) == 1:
        o_refs[0][...] = y
    else:
        for o_ref, count, offset in zip(o_refs, *out_tiles):
            @pl.when((i >= offset) & (i < offset + count))
            def _(o_ref=o_ref):
                o_ref[...] = y


def _matmul_residual(a, b, res_parts, *, out_rows=None, tm, tn, name):
    m, k = a.shape
    _, n = b.shape
    nj = n // tn
    res_tiles = _stack_tiles(res_parts, tm)
    out_rows = out_rows or (m,)
    out_tiles = [r // tm for r in out_rows], [sum(out_rows[:q]) // tm for q in range(len(out_rows))]
    col = lambda j: j
    if len(out_rows) == 1:
        out_specs = pl.BlockSpec((tm, tn), lambda i, j: (i, j))
        out_shape = jax.ShapeDtypeStruct((m, n), F32)
    else:
        out_specs = [_stack_spec((tm, tn), c, o, col, nj) for c, o in zip(*out_tiles)]
        out_shape = [jax.ShapeDtypeStruct((r, n), F32) for r in out_rows]
    return pl.pallas_call(
        functools.partial(_mm_res_kernel, n_res=len(res_parts), res_tiles=res_tiles, out_tiles=out_tiles),
        grid=(m // tm, nj),
        in_specs=[pl.BlockSpec((tm, k), lambda i, j: (i, 0)),
                  pl.BlockSpec((k, tn), lambda i, j: (0, j))]
                 + [_stack_spec((tm, tn), c, o, col, nj) for c, o in zip(*res_tiles)],
        out_specs=out_specs,
        out_shape=out_shape,
        compiler_params=_params("arbitrary", "arbitrary"),
        name=name,
    )(a, b, *res_parts)


def _pool_kernel(u_ref, prev_ref, next_ref, w_ref, scale_ref, o_ref, *, tm, seq):
    i = pl.program_id(0)
    g = pl.program_id(1)
    half = jnp.left_shift(1, g)
    pos0 = (i * tm) % seq
    u = u_ref[...]
    prev = jnp.where(pos0 > 0, prev_ref[...], 0.0)
    nxt = jnp.where(pos0 + tm < seq, next_ref[...], 0.0)
    ext = jnp.concatenate([prev, u, nxt], axis=0)
    rr = lax.broadcasted_iota(jnp.int32, (tm, tm + 2 * POOL_HALO), 0)
    cc = lax.broadcasted_iota(jnp.int32, (tm, tm + 2 * POOL_HALO), 1) - POOL_HALO
    band = jnp.where((cc >= rr - half) & (cc < rr + half), 1.0, 0.0).astype(BF16)
    hi = ext.astype(BF16)
    lo = (ext - hi.astype(F32)).astype(BF16)
    wsum = (jnp.dot(band, hi, preferred_element_type=F32)
            + jnp.dot(band, lo, preferred_element_type=F32))
    t = pos0 + lax.broadcasted_iota(jnp.int32, (tm, 1), 0)
    cnt = (jnp.minimum(t + half, seq) - jnp.maximum(t - half, 0)).astype(F32)
    pooled = wsum / cnt - u
    y = jnp.dot(pooled.astype(BF16), w_ref[0], preferred_element_type=F32)
    o_ref[...] = (y * scale_ref[...]).astype(o_ref.dtype)


def _pool_mixer(zs, pool_w, pool_scale, *, seq, tm=512):
    m = zs.shape[0]
    ng, c, _ = pool_w.shape
    assert POOL_WINDOWS == tuple(2 << g for g in range(ng)) and max(POOL_WINDOWS) // 2 <= POOL_HALO
    hb = tm // POOL_HALO
    last = m // POOL_HALO - 1
    return pl.pallas_call(
        functools.partial(_pool_kernel, tm=tm, seq=seq),
        grid=(m // tm, ng),
        in_specs=[pl.BlockSpec((tm, c), lambda i, g: (i, g)),
                  pl.BlockSpec((POOL_HALO, c), lambda i, g: (jnp.maximum(i * hb - 1, 0), g)),
                  pl.BlockSpec((POOL_HALO, c), lambda i, g: (jnp.minimum((i + 1) * hb, last), g)),
                  pl.BlockSpec((1, c, c), lambda i, g: (g, 0, 0)),
                  pl.BlockSpec((1, c), lambda i, g: (0, g))],
        out_specs=pl.BlockSpec((tm, c), lambda i, g: (i, g)),
        out_shape=jax.ShapeDtypeStruct((m, ng * c), BF16),
        compiler_params=_params("parallel", "arbitrary"),
        name="pool_mixer",
    )(zs, zs, zs, pool_w, pool_scale.reshape(1, ng * c))


def _fold_rope(t):
    folded = t + pltpu.roll(t, ROPE_DIM, axis=1)
    lane = lax.broadcasted_iota(jnp.int32, t.shape, 1)
    return jnp.where(lane < ROPE_DIM, folded, 0.0)


def _qproj_kernel(c_ref, ga_ref, w_ref, gn_ref, gr_ref, tab_ref, o_ref, *, out_scale):
    x = c_ref[...]
    ms = jnp.mean(x * x, axis=-1, keepdims=True)
    a = (x * lax.rsqrt(ms + RMS_EPS) * ga_ref[...]).astype(BF16)
    gn = gn_ref[...] * out_scale
    rope_mul = gr_ref[...] * tab_ref[...] * out_scale
    for h in range(w_ref.shape[0]):
        y = jnp.dot(a, w_ref[h], preferred_element_type=F32)
        nope = y[:, :NOPE_DIM]
        rr = y[:, NOPE_DIM:]
        ssq = jnp.sum(nope * nope, axis=-1, keepdims=True) + 0.5 * jnp.sum(rr * rr, axis=-1, keepdims=True)
        r = lax.rsqrt(ssq * (1.0 / QK_DIM) + RMS_EPS)
        o_ref[0, h, :, :NOPE_DIM] = (nope * r * gn).astype(o_ref.dtype)
        o_ref[0, h, :, NOPE_DIM:] = _fold_rope(rr * r * rope_mul).astype(o_ref.dtype)


def _q_proj(zs, col_block, ga, wq, gn, gr, tab, *, n_seq, seq, out_scale, tm=512):
    m = zs.shape[0]
    nh, kq, _ = wq.shape
    spt = seq // tm
    return pl.pallas_call(
        functools.partial(_qproj_kernel, out_scale=out_scale),
        grid=(m // tm,),
        in_specs=[pl.BlockSpec((tm, kq), lambda i: (i, col_block)),
                  pl.BlockSpec((1, kq), lambda i: (0, 0)),
                  pl.BlockSpec((nh, kq, HEAD_PAD), lambda i: (0, 0, 0)),
                  pl.BlockSpec((1, NOPE_DIM), lambda i: (0, 0)),
                  pl.BlockSpec((1, 2 * ROPE_DIM), lambda i: (0, 0)),
                  pl.BlockSpec((tm, 2 * ROPE_DIM), lambda i: (i % spt, 0))],
        out_specs=pl.BlockSpec((1, nh, tm, HEAD_PAD), lambda i: (i // spt, 0, i % spt, 0)),
        out_shape=jax.ShapeDtypeStruct((n_seq, nh, seq, HEAD_PAD), BF16),
        compiler_params=_params("parallel"),
        name="q_proj",
    )(zs, ga, wq, gn, gr, tab)


def _kvproj_kernel(c_ref, kr_ref, ga_ref, wk_ref, wvt_ref, gn_ref, gr_ref, tab_ref, k_ref, vt_ref):
    x = c_ref[...]
    ms = jnp.mean(x * x, axis=-1, keepdims=True)
    a = (x * lax.rsqrt(ms + RMS_EPS) * ga_ref[...]).astype(BF16)
    kr = kr_ref[...]
    ssq_rope = 0.5 * jnp.sum(kr * kr, axis=-1, keepdims=True)
    rope = _fold_rope(kr * gr_ref[...] * tab_ref[...])
    gn = gn_ref[...]
    pad_shape = (vt_ref.shape[3] - V_DIM, vt_ref.shape[4])
    ones_rows = jnp.where(lax.broadcasted_iota(jnp.int32, pad_shape, 0) == 0, 1.0, 0.0).astype(vt_ref.dtype)
    for h in range(wk_ref.shape[0]):
        nope = jnp.dot(a, wk_ref[h], preferred_element_type=F32)
        ssq = jnp.sum(nope * nope, axis=-1, keepdims=True) + ssq_rope
        r = lax.rsqrt(ssq * (1.0 / QK_DIM) + RMS_EPS)
        k_ref[0, h, :, :NOPE_DIM] = (nope * r * gn).astype(k_ref.dtype)
        k_ref[0, h, :, NOPE_DIM:] = (rope * r).astype(k_ref.dtype)
        vt = lax.dot_general(wvt_ref[h], a, (((1,), (1,)), ((), ())), preferred_element_type=F32)
        vt_ref[0, h, 0, :V_DIM] = vt.astype(vt_ref.dtype)
        vt_ref[0, h, 0, V_DIM:] = ones_rows


def _kv_proj(zs, ckv_block, kr_block, ga, wk, wvt, gn, gr, tab, *, n_seq, seq, tm):
    m = zs.shape[0]
    nh, kc, _ = wk.shape
    spt = seq // tm
    return pl.pallas_call(
        _kvproj_kernel,
        grid=(m // tm,),
        in_specs=[pl.BlockSpec((tm, kc), lambda i: (i, ckv_block)),
                  pl.BlockSpec((tm, 2 * ROPE_DIM), lambda i: (i, kr_block)),
                  pl.BlockSpec((1, kc), lambda i: (0, 0)),
                  pl.BlockSpec((nh, kc, NOPE_DIM), lambda i: (0, 0, 0)),
                  pl.BlockSpec((nh, V_DIM, kc), lambda i: (0, 0, 0)),
                  pl.BlockSpec((1, NOPE_DIM), lambda i: (0, 0)),
                  pl.BlockSpec((1, 2 * ROPE_DIM), lambda i: (0, 0)),
                  pl.BlockSpec((tm, 2 * ROPE_DIM), lambda i: (i % spt, 0))],
        out_specs=[pl.BlockSpec((1, nh, tm, HEAD_PAD), lambda i: (i // spt, 0, i % spt, 0)),
                   pl.BlockSpec((1, nh, 1, V_ROWS, tm), lambda i: (i // spt, 0, i % spt, 0, 0))],
        out_shape=[jax.ShapeDtypeStruct((n_seq, nh, seq, HEAD_PAD), BF16),
                   jax.ShapeDtypeStruct((n_seq, nh, spt, V_ROWS, tm), BF16)],
        compiler_params=_params("parallel"),
        name="kv_proj",
    )(zs, zs, ga, wk, wvt, gn, gr, tab)


def _attn_kernel(q_ref, k_ref, vt_ref, o_ref, s_even, s_odd, acc_scr):
    tk, tq = s_even.shape
    stored = vt_ref.shape[4]
    per_step = tk // stored
    nk = vt_ref.shape[2] // per_step
    nq = q_ref.shape[2] // tq
    assert nk % 2 == 0 and nk >= 2

    def q_tile(i):
        return q_ref[0, 0, pl.ds(pl.multiple_of(i * tq, tq), tq), :]

    def scores(q, j, dst):
        k = k_ref[0, 0, pl.ds(pl.multiple_of(j * tk, tk), tk), :]
        dst[...] = lax.dot_general(k, q, (((1,), (1,)), ((), ())), preferred_element_type=F32)

    def update(j, src, m):
        st = src[...]
        m_new = jnp.maximum(m, jnp.max(st, axis=0, keepdims=True))
        alpha = jnp.exp2(m - m_new)
        p = jnp.exp2(st - m_new).astype(BF16)
        vt = jnp.concatenate([vt_ref[0, 0, j * per_step + c] for c in range(per_step)], axis=1)
        pv = jnp.dot(vt, p, preferred_element_type=F32)
        acc_scr[...] = alpha * acc_scr[...] + pv
        return m_new

    scores(q_tile(0), 0, s_even)

    unroll = min(ATTN_UNROLL, nk)
    assert unroll % 2 == 0 and nk % unroll == 0
    bufs = (s_even, s_odd)

    def per_q_tile(i, _):
        q = q_tile(i)
        acc_scr[...] = jnp.zeros_like(acc_scr)

        def chunk_group(j0, m, q_after):
            for u in range(unroll):
                if u + 1 < unroll:
                    scores(q, j0 + u + 1, bufs[(u + 1) % 2])
                else:
                    q_next, j_next = q_after
                    scores(q_next, j_next, bufs[0])
                m = update(j0 + u, bufs[u % 2], m)
            return m

        m = lax.fori_loop(
            0, nk // unroll - 1,
            lambda g, c: chunk_group(g * unroll, c, (q, (g + 1) * unroll)),
            jnp.full((1, tq), -jnp.inf, F32))
        chunk_group(nk - unroll, m, (q_tile(jnp.minimum(i + 1, nq - 1)), 0))
        out = acc_scr[:V_DIM] / acc_scr[V_DIM:V_DIM + 1]
        o_ref[0, pl.ds(pl.multiple_of(i * tq, tq), tq), :] = out.T.astype(o_ref.dtype)
        return 0

    lax.fori_loop(0, nq, per_q_tile, 0)


def _attention(q, k, vt, *, tq=1024, tk=2048):
    n_seq, nh, seq, _ = q.shape
    nk, v_rows, stored = vt.shape[2:]
    assert tk % stored == 0
    return pl.pallas_call(
        _attn_kernel,
        grid=(n_seq, nh),
        in_specs=[pl.BlockSpec((1, 1, seq, HEAD_PAD), lambda b, h: (b, h, 0, 0)),
                  pl.BlockSpec((1, 1, seq, HEAD_PAD), lambda b, h: (b, h, 0, 0)),
                  pl.BlockSpec((1, 1, nk, v_rows, stored), lambda b, h: (b, h, 0, 0, 0))],
        out_specs=pl.BlockSpec((1, seq, V_DIM), lambda b, h: (b, 0, h)),
        out_shape=jax.ShapeDtypeStruct((n_seq, seq, nh * V_DIM), BF16),
        scratch_shapes=[pltpu.VMEM((tk, tq), F32), pltpu.VMEM((tk, tq), F32), pltpu.VMEM((v_rows, tq), F32)],
        compiler_params=_params("parallel", "arbitrary"),
        name="attention",
    )(q, k, vt)


def _merge_kernel(a_ref, b_ref, wa_ref, wb_ref, ga_ref, gb_ref, o_ref):
    pa = jnp.dot(a_ref[...], wa_ref[...], preferred_element_type=F32)
    pb = jnp.dot(b_ref[...], wb_ref[...], preferred_element_type=F32)
    o_ref[...] = (ga_ref[...].astype(F32) * pa + gb_ref[...].astype(F32) * pb).astype(o_ref.dtype)


def _gated_merge(a, b, wa, wb, gates, *, tm=1024, tn=1024):
    m, k = a.shape
    n = wa.shape[1]
    nj = n // tn
    return pl.pallas_call(
        _merge_kernel,
        grid=(m // tm, nj),
        in_specs=[pl.BlockSpec((tm, k), lambda i, j: (i, 0)),
                  pl.BlockSpec((tm, k), lambda i, j: (i, 0)),
                  pl.BlockSpec((k, tn), lambda i, j: (0, j)),
                  pl.BlockSpec((k, tn), lambda i, j: (0, j)),
                  pl.BlockSpec((tm, tn), lambda i, j: (i, j)),
                  pl.BlockSpec((tm, tn), lambda i, j: (i, j + nj))],
        out_specs=pl.BlockSpec((tm, tn), lambda i, j: (i, j)),
        out_shape=jax.ShapeDtypeStruct((m, n), BF16),
        compiler_params=_params("parallel", "arbitrary"),
        name="gated_merge",
    )(a, b, wa, wb, gates, gates)


def _ffn_up_kernel(x_ref, prev_ref, next_ref, gain_ref, wg_ref, wv_ref, cwg_ref, cwv_ref, cbg_ref, cbv_ref,
                   o_ref, a_scr, *, tm, seq):
    @pl.when(pl.program_id(1) == 0)
    def _():
        def normed(x, keep=None):
            ms = jnp.mean(x * x, axis=-1, keepdims=True)
            h = x * lax.rsqrt(ms + RMS_EPS) * gain_ref[...]
            return (h if keep is None else jnp.where(keep, h, 0.0)).astype(a_scr.dtype)

        pos0 = (pl.program_id(0) * tm) % seq
        a_scr[:CONV_HALO] = normed(prev_ref[...], pos0 > 0)
        a_scr[CONV_HALO + tm:] = normed(next_ref[...], pos0 + tm < seq)
        for r in range(0, tm, FFN_NORM_ROWS):
            a_scr[CONV_HALO + r:CONV_HALO + r + FFN_NORM_ROWS] = normed(x_ref[r:r + FFN_NORM_ROWS])

    ext = tm + 2 * CONV_HALO
    chunk = ext // FFN_ROW_CHUNKS
    assert chunk * FFN_ROW_CHUNKS == ext and chunk % 16 == 0

    def conv(u, centre, n, cw_ref, cb_ref):
        cw = cw_ref[...]
        rows = u.shape[0]
        before = pltpu.roll(u, 1, 0)[centre:centre + n]
        after = pltpu.roll(u, rows - 1, 0)[centre:centre + n]
        return before * cw[0:1] + u[centre:centre + n] * cw[1:2] + after * cw[2:3] + cb_ref[...]

    ug = uv = None
    have_lo = 0
    done = 0
    for c in range(FFN_ROW_CHUNKS):
        a = a_scr[c * chunk:(c + 1) * chunk]
        ug_c = jnp.dot(a, wg_ref[...], preferred_element_type=F32)
        uv_c = jnp.dot(a, wv_ref[...], preferred_element_type=F32)
        ug = ug_c if ug is None else jnp.concatenate([ug, ug_c], axis=0)
        uv = uv_c if uv is None else jnp.concatenate([uv, uv_c], axis=0)
        upto = tm if c == FFN_ROW_CHUNKS - 1 else ((c + 1) * chunk - CONV_HALO - 1) // 16 * 16
        n = upto - done
        centre = done + CONV_HALO - have_lo
        gate = conv(ug, centre, n, cwg_ref, cbg_ref)
        val = conv(uv, centre, n, cwv_ref, cbv_ref)
        half_gate = 0.5 * gate
        o_ref[done:upto] = ((half_gate * jnp.tanh(half_gate) + half_gate) * val).astype(o_ref.dtype)
        keep_lo = (upto + CONV_HALO - 1) // 8 * 8
        ug, uv = ug[keep_lo - have_lo:], uv[keep_lo - have_lo:]
        have_lo, done = keep_lo, upto


def _ffn_up(x, gain, w_up, conv_w, conv_b, *, seq, tm=1024, tn=256):
    m, d = x.shape
    dff = w_up.shape[1] // 2
    nj = dff // tn
    hb = tm // CONV_HALO
    last = m // CONV_HALO - 1
    return pl.pallas_call(
        functools.partial(_ffn_up_kernel, tm=tm, seq=seq),
        grid=(m // tm, nj),
        in_specs=[pl.BlockSpec((tm, d), lambda i, j: (i, 0)),
                  pl.BlockSpec((CONV_HALO, d), lambda i, j: (jnp.maximum(i * hb - 1, 0), 0)),
                  pl.BlockSpec((CONV_HALO, d), lambda i, j: (jnp.minimum((i + 1) * hb, last), 0)),
                  pl.BlockSpec((1, d), lambda i, j: (0, 0)),
                  pl.BlockSpec((d, tn), lambda i, j: (0, j)),
                  pl.BlockSpec((d, tn), lambda i, j: (0, j + nj)),
                  pl.BlockSpec((3, tn), lambda i, j: (0, j)),
                  pl.BlockSpec((3, tn), lambda i, j: (0, j + nj)),
                  pl.BlockSpec((1, tn), lambda i, j: (0, j)),
                  pl.BlockSpec((1, tn), lambda i, j: (0, j + nj))],
        out_specs=pl.BlockSpec((tm, tn), lambda i, j: (i, j)),
        out_shape=jax.ShapeDtypeStruct((m, dff), BF16),
        scratch_shapes=[pltpu.VMEM((tm + 2 * CONV_HALO, d), BF16)],
        compiler_params=_params("parallel", "arbitrary"),
        name="ffn_up",
    )(x, x, x, gain.reshape(1, d), w_up, w_up, conv_w, conv_w, conv_b, conv_b)


def _swap_halves(a):
    h = a.shape[-1] // 2
    return jnp.concatenate([a[..., h:], a[..., :h]], axis=-1)


def _layer(x_parts, n_seq, seq, norm_mix_gain, w_in, pool_w, pool_scale, q_a_norm_gain, w_uq,
           kv_a_norm_gain, w_ukv, q_norm_gain, k_norm_gain, w_branch_pool, w_branch_mla, w_o,
           norm_ffn_gain, w_up, conv_w, conv_b, w_down):
    d = x_parts[0].shape[1]
    ng, pc, _ = pool_w.shape
    pool_width = ng * pc
    q_lora = q_a_norm_gain.shape[0]
    kv_lora = kv_a_norm_gain.shape[0]
    off_cq = pool_width
    off_ckv = off_cq + q_lora
    off_kr = off_ckv + kv_lora
    off_gp = off_kr + ROPE_DIM

    w_kr = w_in[:, off_kr:off_gp]
    small_cols = off_kr + 2 * ROPE_DIM
    small_pad = (-small_cols) % IN_PROJ_SMALL_TN
    w_small = jnp.concatenate(
        [w_in[:, :off_kr], w_kr, _swap_halves(w_kr), jnp.zeros((d, small_pad), F32)], axis=1).astype(BF16)
    w_gate = w_in[:, off_gp:].astype(BF16)

    wq = w_uq.reshape(q_lora, N_HEADS, QK_DIM)
    wq = jnp.concatenate([wq, _swap_halves(wq[..., NOPE_DIM:])], axis=-1)
    wq = wq.transpose(1, 0, 2).astype(BF16)
    wkv = w_ukv.reshape(kv_lora, N_HEADS, NOPE_DIM + V_DIM).astype(BF16)
    wk = wkv[..., :NOPE_DIM].transpose(1, 0, 2)
    wvt = wkv[..., NOPE_DIM:].transpose(1, 2, 0)

    def gains(g):
        gr = g[NOPE_DIM:]
        return g[:NOPE_DIM].reshape(1, NOPE_DIM), jnp.concatenate([gr, _swap_halves(gr)]).reshape(1, 2 * ROPE_DIM)

    qgn, qgr = gains(q_norm_gain)
    kgn, kgr = gains(k_norm_gain)

    inv = 1.0 / (ROPE_THETA ** (jnp.arange(0, ROPE_DIM, 2, dtype=F32) / ROPE_DIM))
    ang = jnp.arange(seq, dtype=F32)[:, None] * inv[None, :]
    cos, sin = jnp.cos(ang), jnp.sin(ang)
    rope_tab = jnp.concatenate([cos, cos, -sin, sin], axis=1)

    h = _rmsnorm(x_parts, norm_mix_gain)
    zs = _matmul(h, w_small, tm=1024, tn=IN_PROJ_SMALL_TN, out_dtype=F32, name="in_proj_small")
    gates = _matmul(h, w_gate, tm=1024, tn=1024, out_dtype=BF16, gate=True, name="in_proj_gates")

    a_out = _pool_mixer(zs, pool_w.astype(BF16), pool_scale, seq=seq)

    q_scale = math.log2(math.e) / math.sqrt(QK_DIM)
    q = _q_proj(zs, off_cq // q_lora, q_a_norm_gain.reshape(1, q_lora), wq, qgn, qgr, rope_tab,
                n_seq=n_seq, seq=seq, out_scale=q_scale)
    k, vt = _kv_proj(zs, off_ckv // kv_lora, off_kr // (2 * ROPE_DIM), kv_a_norm_gain.reshape(1, kv_lora),
                     wk, wvt, kgn, kgr, rope_tab, n_seq=n_seq, seq=seq, tm=ATTN_KEY_TILE)
    b_out = _attention(q, k, vt).reshape(n_seq * seq, N_HEADS * V_DIM)

    merged = _gated_merge(a_out, b_out, w_branch_pool.astype(BF16), w_branch_mla.astype(BF16), gates)
    x1 = _matmul_residual(merged, w_o.astype(BF16), x_parts, tm=1024, tn=512, name="out_proj")

    act = _ffn_up(x1, norm_ffn_gain, w_up.astype(BF16), conv_w, conv_b.reshape(1, -1), seq=seq)
    return _matmul_residual(act, w_down.astype(BF16), (x1,), out_rows=tuple(p.shape[0] for p in x_parts),
                            tm=512, tn=512, name="down_proj")


def kernel(x_prompt, x_sample, norm_mix_gain, w_in, pool_w, pool_scale, q_a_norm_gain, w_uq,
           kv_a_norm_gain, w_ukv, q_norm_gain, k_norm_gain, w_branch_pool, w_branch_mla, w_o,
           norm_ffn_gain, w_up, conv_w, conv_b, w_down):
    bp, seq, d = x_prompt.shape
    bs = x_sample.shape[0]
    assert x_sample.shape[1:] == (seq, d)
    n_seq = bp + bs
    y = (x_prompt.reshape(bp * seq, d), x_sample.reshape(bs * seq, d))
    for l in range(norm_mix_gain.shape[0]):
        y = _layer(y, n_seq, seq, norm_mix_gain[l], w_in[l], pool_w[l], pool_scale[l], q_a_norm_gain[l],
                   w_uq[l], kv_a_norm_gain[l], w_ukv[l], q_norm_gain[l], k_norm_gain[l],
                   w_branch_pool[l], w_branch_mla[l], w_o[l], norm_ffn_gain[l], w_up[l],
                   conv_w[l], conv_b[l], w_down[l])
    return (y[0].reshape(bp, seq, d), y[1].reshape(bs, seq, d))
```

```python
import functools
import math

import jax
import jax.numpy as jnp
from jax import lax
from jax.experimental import pallas as pl
from jax.experimental.pallas import tpu as pltpu

F32 = jnp.float32
BF16 = jnp.bfloat16

RMS_EPS = 1e-6
ROPE_THETA = 10000.0
N_HEADS = 16
NOPE_DIM = 128
ROPE_DIM = 64
V_DIM = 128
QK_DIM = NOPE_DIM + ROPE_DIM
HEAD_PAD = 256
V_ROWS = V_DIM + 16
POOL_WINDOWS = (2, 4, 8, 16)
POOL_HALO = 8
CONV_HALO = 16
SIDE_CAST_ROWS_UP = 32
SIDE_CAST_ROWS_DOWN = 64
IN_PROJ_SMALL_TN = 1280
FFN_NORM_ROWS = 128
FFN_ROW_CHUNKS = 2
ATTN_UNROLL = 16
ATTN_KEY_TILE = 1024

V7X_VMEM_LIMIT = 56 * 1024 * 1024


def _params(*sem):
    return pltpu.CompilerParams(dimension_semantics=sem, vmem_limit_bytes=V7X_VMEM_LIMIT)


def _stack_tiles(parts, tm):
    counts = [p.shape[0] // tm for p in parts]
    offsets = [sum(counts[:k]) for k in range(len(parts))]
    return counts, offsets


def _stack_spec(block, count, offset, col=None, ncols=1):
    def index(i, *j):
        row = jnp.clip(i - offset, 0, count - 1)
        if col is None:
            return row, 0
        active = (i >= offset) & (i < offset + count)
        frozen = jnp.where(i < offset, 0, ncols - 1)
        return row, jnp.where(active, col(*j), frozen)
    return pl.BlockSpec(block, index)


def _stack_read(refs, counts, offsets, i):
    x = refs[-1][...]
    for k in range(len(refs) - 2, -1, -1):
        x = jnp.where(i < offsets[k] + counts[k], refs[k][...], x)
    return x


def _rmsnorm_kernel(*refs, counts, offsets):
    *x_refs, g_ref, o_ref = refs
    x = _stack_read(x_refs, counts, offsets, pl.program_id(0))
    ms = jnp.mean(x * x, axis=-1, keepdims=True)
    o_ref[...] = (x * lax.rsqrt(ms + RMS_EPS) * g_ref[...]).astype(o_ref.dtype)


def _rmsnorm(parts, gain, tr=256):
    d = parts[0].shape[1]
    counts, offsets = _stack_tiles(parts, tr)
    return pl.pallas_call(
        functools.partial(_rmsnorm_kernel, counts=counts, offsets=offsets),
        grid=(sum(counts),),
        in_specs=[_stack_spec((tr, d), c, o) for c, o in zip(counts, offsets)]
                 + [pl.BlockSpec((1, d), lambda i: (0, 0))],
        out_specs=pl.BlockSpec((tr, d), lambda i: (i, 0)),
        out_shape=jax.ShapeDtypeStruct((sum(counts) * tr, d), BF16),
        compiler_params=_params("arbitrary"),
        name="rmsnorm",
    )(*parts, gain.reshape(1, d))


def _sigmoid(x):
    return 0.5 * jnp.tanh(0.5 * x) + 0.5


def _side_specs(side, steps, nj):
    if side is None:
        return [], [], []
    w, rows = side
    nblk = w.shape[0] // rows
    assert nblk * rows == w.shape[0] and rows % 16 == 0 and nblk <= steps
    spec = pl.BlockSpec((rows, w.shape[1]), lambda i, j: (jnp.minimum(i * nj + j, nblk - 1), 0))
    return [spec], [spec], [jax.ShapeDtypeStruct(w.shape, BF16)]


def _mm_kernel(*refs, gate, has_side):
    a_ref, b_ref = refs[:2]
    if has_side:
        side_in, o_ref, side_out = refs[2:]
        side_out[...] = side_in[...].astype(side_out.dtype)
    else:
        o_ref, = refs[2:]
    acc = jnp.dot(a_ref[...], b_ref[...], preferred_element_type=F32)
    if gate:
        acc = _sigmoid(acc)
    o_ref[...] = acc.astype(o_ref.dtype)


def _matmul(a, b, *, tm, tn, out_dtype, gate=False, side=None, name):
    m, k = a.shape
    _, n = b.shape
    nj = n // tn
    side_in, side_out, side_shape = _side_specs(side, (m // tm) * nj, nj)
    out = pl.pallas_call(
        functools.partial(_mm_kernel, gate=gate, has_side=side is not None),
        grid=(m // tm, nj),
        in_specs=[pl.BlockSpec((tm, k), lambda i, j: (i, 0)),
                  pl.BlockSpec((k, tn), lambda i, j: (0, j))] + side_in,
        out_specs=[pl.BlockSpec((tm, tn), lambda i, j: (i, j))] + side_out,
        out_shape=[jax.ShapeDtypeStruct((m, n), out_dtype)] + side_shape,
        compiler_params=_params("arbitrary", "arbitrary"),
        name=name,
    )(a, b, *([side[0]] if side else []))
    return tuple(out) if side else out[0]


def _mm_res_kernel(*refs, n_res, res_tiles, out_tiles, has_side):
    a_ref, b_ref = refs[:2]
    r_refs = refs[2:2 + n_res]
    if has_side:
        side_in, *o_refs, side_out = refs[2 + n_res:]
        side_out[...] = side_in[...].astype(side_out.dtype)
    else:
        o_refs = refs[2 + n_res:]
    i = pl.program_id(0)
    acc = jnp.dot(a_ref[...], b_ref[...], preferred_element_type=F32)
    y = _stack_read(r_refs, *res_tiles, i) + acc
    if len(o_refs) == 1:
        o_refs[0][...] = y
    else:
        for o_ref, count, offset in zip(o_refs, *out_tiles):
            @pl.when((i >= offset) & (i < offset + count))
            def _(o_ref=o_ref):
                o_ref[...] = y


def _matmul_residual(a, b, res_parts, *, out_rows=None, side=None, tm, tn, name):
    m, k = a.shape
    _, n = b.shape
    nj = n // tn
    res_tiles = _stack_tiles(res_parts, tm)
    out_rows = out_rows or (m,)
    out_tiles = [r // tm for r in out_rows], [sum(out_rows[:q]) // tm for q in range(len(out_rows))]
    col = lambda j: j
    if len(out_rows) == 1:
        out_specs = [pl.BlockSpec((tm, tn), lambda i, j: (i, j))]
        out_shape = [jax.ShapeDtypeStruct((m, n), F32)]
    else:
        out_specs = [_stack_spec((tm, tn), c, o, col, nj) for c, o in zip(*out_tiles)]
        out_shape = [jax.ShapeDtypeStruct((r, n), F32) for r in out_rows]
    side_in, side_out, side_shape = _side_specs(side, (m // tm) * nj, nj)
    out = pl.pallas_call(
        functools.partial(_mm_res_kernel, n_res=len(res_parts), res_tiles=res_tiles, out_tiles=out_tiles,
                          has_side=side is not None),
        grid=(m // tm, nj),
        in_specs=[pl.BlockSpec((tm, k), lambda i, j: (i, 0)),
                  pl.BlockSpec((k, tn), lambda i, j: (0, j))]
                 + [_stack_spec((tm, tn), c, o, col, nj) for c, o in zip(*res_tiles)] + side_in,
        out_specs=out_specs + side_out,
        out_shape=out_shape + side_shape,
        compiler_params=_params("arbitrary", "arbitrary"),
        name=name,
    )(a, b, *res_parts, *([side[0]] if side else []))
    main = out[0] if len(out_rows) == 1 else tuple(out[:len(out_rows)])
    return (main, out[-1]) if side else main


def _pool_kernel(u_ref, prev_ref, next_ref, w_ref, scale_ref, o_ref, *, tm, seq):
    i = pl.program_id(0)
    g = pl.program_id(1)
    half = jnp.left_shift(1, g)
    pos0 = (i * tm) % seq
    u = u_ref[...]
    prev = jnp.where(pos0 > 0, prev_ref[...], 0.0)
    nxt = jnp.where(pos0 + tm < seq, next_ref[...], 0.0)
    ext = jnp.concatenate([prev, u, nxt], axis=0)
    rr = lax.broadcasted_iota(jnp.int32, (tm, tm + 2 * POOL_HALO), 0)
    cc = lax.broadcasted_iota(jnp.int32, (tm, tm + 2 * POOL_HALO), 1) - POOL_HALO
    band = jnp.where((cc >= rr - half) & (cc < rr + half), 1.0, 0.0).astype(BF16)
    hi = ext.astype(BF16)
    lo = (ext - hi.astype(F32)).astype(BF16)
    wsum = (jnp.dot(band, hi, preferred_element_type=F32)
            + jnp.dot(band, lo, preferred_element_type=F32))
    t = pos0 + lax.broadcasted_iota(jnp.int32, (tm, 1), 0)
    cnt = (jnp.minimum(t + half, seq) - jnp.maximum(t - half, 0)).astype(F32)
    pooled = wsum / cnt - u
    y = jnp.dot(pooled.astype(BF16), w_ref[0], preferred_element_type=F32)
    o_ref[...] = (y * scale_ref[...]).astype(o_ref.dtype)


def _pool_mixer(zs, pool_w, pool_scale, *, seq, tm=512):
    m = zs.shape[0]
    ng, c, _ = pool_w.shape
    assert POOL_WINDOWS == tuple(2 << g for g in range(ng)) and max(POOL_WINDOWS) // 2 <= POOL_HALO
    hb = tm // POOL_HALO
    last = m // POOL_HALO - 1
    return pl.pallas_call(
        functools.partial(_pool_kernel, tm=tm, seq=seq),
        grid=(m // tm, ng),
        in_specs=[pl.BlockSpec((tm, c), lambda i, g: (i, g)),
                  pl.BlockSpec((POOL_HALO, c), lambda i, g: (jnp.maximum(i * hb - 1, 0), g)),
                  pl.BlockSpec((POOL_HALO, c), lambda i, g: (jnp.minimum((i + 1) * hb, last), g)),
                  pl.BlockSpec((1, c, c), lambda i, g: (g, 0, 0)),
                  pl.BlockSpec((1, c), lambda i, g: (0, g))],
        out_specs=pl.BlockSpec((tm, c), lambda i, g: (i, g)),
        out_shape=jax.ShapeDtypeStruct((m, ng * c), BF16),
        compiler_params=_params("parallel", "arbitrary"),
        name="pool_mixer",
    )(zs, zs, zs, pool_w, pool_scale.reshape(1, ng * c))


def _fold_rope(t):
    folded = t + pltpu.roll(t, ROPE_DIM, axis=1)
    lane = lax.broadcasted_iota(jnp.int32, t.shape, 1)
    return jnp.where(lane < ROPE_DIM, folded, 0.0)


def _qproj_kernel(c_ref, ga_ref, w_ref, gn_ref, gr_ref, tab_ref, o_ref, *, out_scale):
    x = c_ref[...]
    ms = jnp.mean(x * x, axis=-1, keepdims=True)
    a = (x * lax.rsqrt(ms + RMS_EPS) * ga_ref[...]).astype(BF16)
    gn = gn_ref[...] * out_scale
    rope_mul = gr_ref[...] * tab_ref[...] * out_scale
    for h in range(w_ref.shape[0]):
        y = jnp.dot(a, w_ref[h], preferred_element_type=F32)
        nope = y[:, :NOPE_DIM]
        rr = y[:, NOPE_DIM:]
        ssq = jnp.sum(nope * nope, axis=-1, keepdims=True) + 0.5 * jnp.sum(rr * rr, axis=-1, keepdims=True)
        r = lax.rsqrt(ssq * (1.0 / QK_DIM) + RMS_EPS)
        o_ref[0, h, :, :NOPE_DIM] = (nope * r * gn).astype(o_ref.dtype)
        o_ref[0, h, :, NOPE_DIM:] = _fold_rope(rr * r * rope_mul).astype(o_ref.dtype)


def _q_proj(zs, col_block, ga, wq, gn, gr, tab, *, n_seq, seq, out_scale, tm=512):
    m = zs.shape[0]
    nh, kq, _ = wq.shape
    spt = seq // tm
    return pl.pallas_call(
        functools.partial(_qproj_kernel, out_scale=out_scale),
        grid=(m // tm,),
        in_specs=[pl.BlockSpec((tm, kq), lambda i: (i, col_block)),
                  pl.BlockSpec((1, kq), lambda i: (0, 0)),
                  pl.BlockSpec((nh, kq, HEAD_PAD), lambda i: (0, 0, 0)),
                  pl.BlockSpec((1, NOPE_DIM), lambda i: (0, 0)),
                  pl.BlockSpec((1, 2 * ROPE_DIM), lambda i: (0, 0)),
                  pl.BlockSpec((tm, 2 * ROPE_DIM), lambda i: (i % spt, 0))],
        out_specs=pl.BlockSpec((1, nh, tm, HEAD_PAD), lambda i: (i // spt, 0, i % spt, 0)),
        out_shape=jax.ShapeDtypeStruct((n_seq, nh, seq, HEAD_PAD), BF16),
        compiler_params=_params("parallel"),
        name="q_proj",
    )(zs, ga, wq, gn, gr, tab)


def _kvproj_kernel(c_ref, kr_ref, ga_ref, wk_ref, wvt_ref, gn_ref, gr_ref, tab_ref, k_ref, vt_ref):
    x = c_ref[...]
    ms = jnp.mean(x * x, axis=-1, keepdims=True)
    a = (x * lax.rsqrt(ms + RMS_EPS) * ga_ref[...]).astype(BF16)
    kr = kr_ref[...]
    ssq_rope = 0.5 * jnp.sum(kr * kr, axis=-1, keepdims=True)
    rope = _fold_rope(kr * gr_ref[...] * tab_ref[...])
    gn = gn_ref[...]
    pad_shape = (vt_ref.shape[3] - V_DIM, vt_ref.shape[4])
    ones_rows = jnp.where(lax.broadcasted_iota(jnp.int32, pad_shape, 0) == 0, 1.0, 0.0).astype(vt_ref.dtype)
    for h in range(wk_ref.shape[0]):
        nope = jnp.dot(a, wk_ref[h], preferred_element_type=F32)
        ssq = jnp.sum(nope * nope, axis=-1, keepdims=True) + ssq_rope
        r = lax.rsqrt(ssq * (1.0 / QK_DIM) + RMS_EPS)
        k_ref[0, h, :, :NOPE_DIM] = (nope * r * gn).astype(k_ref.dtype)
        k_ref[0, h, :, NOPE_DIM:] = (rope * r).astype(k_ref.dtype)
        vt = lax.dot_general(wvt_ref[h], a, (((1,), (1,)), ((), ())), preferred_element_type=F32)
        vt_ref[0, h, 0, :V_DIM] = vt.astype(vt_ref.dtype)
        vt_ref[0, h, 0, V_DIM:] = ones_rows


def _kv_proj(zs, ckv_block, kr_block, ga, wk, wvt, gn, gr, tab, *, n_seq, seq, tm):
    m = zs.shape[0]
    nh, kc, _ = wk.shape
    spt = seq // tm
    return pl.pallas_call(
        _kvproj_kernel,
        grid=(m // tm,),
        in_specs=[pl.BlockSpec((tm, kc), lambda i: (i, ckv_block)),
                  pl.BlockSpec((tm, 2 * ROPE_DIM), lambda i: (i, kr_block)),
                  pl.BlockSpec((1, kc), lambda i: (0, 0)),
                  pl.BlockSpec((nh, kc, NOPE_DIM), lambda i: (0, 0, 0)),
                  pl.BlockSpec((nh, V_DIM, kc), lambda i: (0, 0, 0)),
                  pl.BlockSpec((1, NOPE_DIM), lambda i: (0, 0)),
                  pl.BlockSpec((1, 2 * ROPE_DIM), lambda i: (0, 0)),
                  pl.BlockSpec((tm, 2 * ROPE_DIM), lambda i: (i % spt, 0))],
        out_specs=[pl.BlockSpec((1, nh, tm, HEAD_PAD), lambda i: (i // spt, 0, i % spt, 0)),
                   pl.BlockSpec((1, nh, 1, V_ROWS, tm), lambda i: (i // spt, 0, i % spt, 0, 0))],
        out_shape=[jax.ShapeDtypeStruct((n_seq, nh, seq, HEAD_PAD), BF16),
                   jax.ShapeDtypeStruct((n_seq, nh, spt, V_ROWS, tm), BF16)],
        compiler_params=_params("parallel"),
        name="kv_proj",
    )(zs, zs, ga, wk, wvt, gn, gr, tab)


def _attn_kernel(q_ref, k_ref, vt_ref, o_ref, s_even, s_odd, acc_scr):
    tk, tq = s_even.shape
    stored = vt_ref.shape[4]
    per_step = tk // stored
    nk = vt_ref.shape[2] // per_step
    nq = q_ref.shape[2] // tq
    assert nk % 2 == 0 and nk >= 2

    def q_tile(i):
        return q_ref[0, 0, pl.ds(pl.multiple_of(i * tq, tq), tq), :]

    def scores(q, j, dst):
        k = k_ref[0, 0, pl.ds(pl.multiple_of(j * tk, tk), tk), :]
        dst[...] = lax.dot_general(k, q, (((1,), (1,)), ((), ())), preferred_element_type=F32)

    def update(j, src, m):
        st = src[...]
        m_new = jnp.maximum(m, jnp.max(st, axis=0, keepdims=True))
        alpha = jnp.exp2(m - m_new)
        p = jnp.exp2(st - m_new).astype(BF16)
        vt = jnp.concatenate([vt_ref[0, 0, j * per_step + c] for c in range(per_step)], axis=1)
        pv = jnp.dot(vt, p, preferred_element_type=F32)
        acc_scr[...] = alpha * acc_scr[...] + pv
        return m_new

    scores(q_tile(0), 0, s_even)

    unroll = min(ATTN_UNROLL, nk)
    assert unroll % 2 == 0 and nk % unroll == 0
    bufs = (s_even, s_odd)

    def per_q_tile(i, _):
        q = q_tile(i)
        acc_scr[...] = jnp.zeros_like(acc_scr)

        def chunk_group(j0, m, q_after):
            for u in range(unroll):
                if u + 1 < unroll:
                    scores(q, j0 + u + 1, bufs[(u + 1) % 2])
                else:
                    q_next, j_next = q_after
                    scores(q_next, j_next, bufs[0])
                m = update(j0 + u, bufs[u % 2], m)
            return m

        m = lax.fori_loop(
            0, nk // unroll - 1,
            lambda g, c: chunk_group(g * unroll, c, (q, (g + 1) * unroll)),
            jnp.full((1, tq), -jnp.inf, F32))
        chunk_group(nk - unroll, m, (q_tile(jnp.minimum(i + 1, nq - 1)), 0))
        out = acc_scr[:V_DIM] / acc_scr[V_DIM:V_DIM + 1]
        o_ref[0, pl.ds(pl.multiple_of(i * tq, tq), tq), :] = out.T.astype(o_ref.dtype)
        return 0

    lax.fori_loop(0, nq, per_q_tile, 0)


def _attention(q, k, vt, *, tq=1024, tk=2048):
    n_seq, nh, seq, _ = q.shape
    nk, v_rows, stored = vt.shape[2:]
    assert tk % stored == 0
    return pl.pallas_call(
        _attn_kernel,
        grid=(n_seq, nh),
        in_specs=[pl.BlockSpec((1, 1, seq, HEAD_PAD), lambda b, h: (b, h, 0, 0)),
                  pl.BlockSpec((1, 1, seq, HEAD_PAD), lambda b, h: (b, h, 0, 0)),
                  pl.BlockSpec((1, 1, nk, v_rows, stored), lambda b, h: (b, h, 0, 0, 0))],
        out_specs=pl.BlockSpec((1, seq, V_DIM), lambda b, h: (b, 0, h)),
        out_shape=jax.ShapeDtypeStruct((n_seq, seq, nh * V_DIM), BF16),
        scratch_shapes=[pltpu.VMEM((tk, tq), F32), pltpu.VMEM((tk, tq), F32), pltpu.VMEM((v_rows, tq), F32)],
        compiler_params=_params("parallel", "arbitrary"),
        name="attention",
    )(q, k, vt)


def _merge_kernel(a_ref, b_ref, wa_ref, wb_ref, ga_ref, gb_ref, o_ref):
    pa = jnp.dot(a_ref[...], wa_ref[...], preferred_element_type=F32)
    pb = jnp.dot(b_ref[...], wb_ref[...], preferred_element_type=F32)
    o_ref[...] = (ga_ref[...].astype(F32) * pa + gb_ref[...].astype(F32) * pb).astype(o_ref.dtype)


def _gated_merge(a, b, wa, wb, gates, *, tm=1024, tn=1024):
    m, k = a.shape
    n = wa.shape[1]
    nj = n // tn
    return pl.pallas_call(
        _merge_kernel,
        grid=(m // tm, nj),
        in_specs=[pl.BlockSpec((tm, k), lambda i, j: (i, 0)),
                  pl.BlockSpec((tm, k), lambda i, j: (i, 0)),
                  pl.BlockSpec((k, tn), lambda i, j: (0, j)),
                  pl.BlockSpec((k, tn), lambda i, j: (0, j)),
                  pl.BlockSpec((tm, tn), lambda i, j: (i, j)),
                  pl.BlockSpec((tm, tn), lambda i, j: (i, j + nj))],
        out_specs=pl.BlockSpec((tm, tn), lambda i, j: (i, j)),
        out_shape=jax.ShapeDtypeStruct((m, n), BF16),
        compiler_params=_params("parallel", "arbitrary"),
        name="gated_merge",
    )(a, b, wa, wb, gates, gates)


def _ffn_up_kernel(x_ref, prev_ref, next_ref, gain_ref, wg_ref, wv_ref, cwg_ref, cwv_ref, cbg_ref, cbv_ref,
                   o_ref, a_scr, *, tm, seq):
    @pl.when(pl.program_id(1) == 0)
    def _():
        def normed(x, keep=None):
            ms = jnp.mean(x * x, axis=-1, keepdims=True)
            h = x * lax.rsqrt(ms + RMS_EPS) * gain_ref[...]
            return (h if keep is None else jnp.where(keep, h, 0.0)).astype(a_scr.dtype)

        pos0 = (pl.program_id(0) * tm) % seq
        a_scr[:CONV_HALO] = normed(prev_ref[...], pos0 > 0)
        a_scr[CONV_HALO + tm:] = normed(next_ref[...], pos0 + tm < seq)
        for r in range(0, tm, FFN_NORM_ROWS):
            a_scr[CONV_HALO + r:CONV_HALO + r + FFN_NORM_ROWS] = normed(x_ref[r:r + FFN_NORM_ROWS])

    ext = tm + 2 * CONV_HALO
    chunk = ext // FFN_ROW_CHUNKS
    assert chunk * FFN_ROW_CHUNKS == ext and chunk % 16 == 0

    def conv(u, centre, n, cw_ref, cb_ref):
        cw = cw_ref[...]
        rows = u.shape[0]
        before = pltpu.roll(u, 1, 0)[centre:centre + n]
        after = pltpu.roll(u, rows - 1, 0)[centre:centre + n]
        return before * cw[0:1] + u[centre:centre + n] * cw[1:2] + after * cw[2:3] + cb_ref[...]

    ug = uv = None
    have_lo = 0
    done = 0
    for c in range(FFN_ROW_CHUNKS):
        a = a_scr[c * chunk:(c + 1) * chunk]
        ug_c = jnp.dot(a, wg_ref[...], preferred_element_type=F32)
        uv_c = jnp.dot(a, wv_ref[...], preferred_element_type=F32)
        ug = ug_c if ug is None else jnp.concatenate([ug, ug_c], axis=0)
        uv = uv_c if uv is None else jnp.concatenate([uv, uv_c], axis=0)
        upto = tm if c == FFN_ROW_CHUNKS - 1 else ((c + 1) * chunk - CONV_HALO - 1) // 16 * 16
        n = upto - done
        centre = done + CONV_HALO - have_lo
        gate = conv(ug, centre, n, cwg_ref, cbg_ref)
        val = conv(uv, centre, n, cwv_ref, cbv_ref)
        half_gate = 0.5 * gate
        o_ref[done:upto] = ((half_gate * jnp.tanh(half_gate) + half_gate) * val).astype(o_ref.dtype)
        keep_lo = (upto + CONV_HALO - 1) // 8 * 8
        ug, uv = ug[keep_lo - have_lo:], uv[keep_lo - have_lo:]
        have_lo, done = keep_lo, upto


def _ffn_up(x, gain, w_up, conv_w, conv_b, *, seq, tm=1024, tn=256):
    m, d = x.shape
    dff = w_up.shape[1] // 2
    nj = dff // tn
    hb = tm // CONV_HALO
    last = m // CONV_HALO - 1
    return pl.pallas_call(
        functools.partial(_ffn_up_kernel, tm=tm, seq=seq),
        grid=(m // tm, nj),
        in_specs=[pl.BlockSpec((tm, d), lambda i, j: (i, 0)),
                  pl.BlockSpec((CONV_HALO, d), lambda i, j: (jnp.maximum(i * hb - 1, 0), 0)),
                  pl.BlockSpec((CONV_HALO, d), lambda i, j: (jnp.minimum((i + 1) * hb, last), 0)),
                  pl.BlockSpec((1, d), lambda i, j: (0, 0)),
                  pl.BlockSpec((d, tn), lambda i, j: (0, j)),
                  pl.BlockSpec((d, tn), lambda i, j: (0, j + nj)),
                  pl.BlockSpec((3, tn), lambda i, j: (0, j)),
                  pl.BlockSpec((3, tn), lambda i, j: (0, j + nj)),
                  pl.BlockSpec((1, tn), lambda i, j: (0, j)),
                  pl.BlockSpec((1, tn), lambda i, j: (0, j + nj))],
        out_specs=pl.BlockSpec((tm, tn), lambda i, j: (i, j)),
        out_shape=jax.ShapeDtypeStruct((m, dff), BF16),
        scratch_shapes=[pltpu.VMEM((tm + 2 * CONV_HALO, d), BF16)],
        compiler_params=_params("parallel", "arbitrary"),
        name="ffn_up",
    )(x, x, x, gain.reshape(1, d), w_up, w_up, conv_w, conv_w, conv_b, conv_b)


def _swap_halves(a):
    h = a.shape[-1] // 2
    return jnp.concatenate([a[..., h:], a[..., :h]], axis=-1)


def _layer(x_parts, n_seq, seq, norm_mix_gain, w_in, pool_w, pool_scale, q_a_norm_gain, w_uq,
           kv_a_norm_gain, w_ukv, q_norm_gain, k_norm_gain, w_branch_pool, w_branch_mla, w_o,
           norm_ffn_gain, w_up, conv_w, conv_b, w_down):
    d = x_parts[0].shape[1]
    ng, pc, _ = pool_w.shape
    pool_width = ng * pc
    q_lora = q_a_norm_gain.shape[0]
    kv_lora = kv_a_norm_gain.shape[0]
    off_cq = pool_width
    off_ckv = off_cq + q_lora
    off_kr = off_ckv + kv_lora
    off_gp = off_kr + ROPE_DIM

    w_kr = w_in[:, off_kr:off_gp]
    small_cols = off_kr + 2 * ROPE_DIM
    small_pad = (-small_cols) % IN_PROJ_SMALL_TN
    w_small = jnp.concatenate(
        [w_in[:, :off_kr], w_kr, _swap_halves(w_kr), jnp.zeros((d, small_pad), F32)], axis=1).astype(BF16)
    w_gate = w_in[:, off_gp:].astype(BF16)

    wq = w_uq.reshape(q_lora, N_HEADS, QK_DIM)
    wq = jnp.concatenate([wq, _swap_halves(wq[..., NOPE_DIM:])], axis=-1)
    wq = wq.transpose(1, 0, 2).astype(BF16)
    wkv = w_ukv.reshape(kv_lora, N_HEADS, NOPE_DIM + V_DIM).astype(BF16)
    wk = wkv[..., :NOPE_DIM].transpose(1, 0, 2)
    wvt = wkv[..., NOPE_DIM:].transpose(1, 2, 0)

    def gains(g):
        gr = g[NOPE_DIM:]
        return g[:NOPE_DIM].reshape(1, NOPE_DIM), jnp.concatenate([gr, _swap_halves(gr)]).reshape(1, 2 * ROPE_DIM)

    qgn, qgr = gains(q_norm_gain)
    kgn, kgr = gains(k_norm_gain)

    inv = 1.0 / (ROPE_THETA ** (jnp.arange(0, ROPE_DIM, 2, dtype=F32) / ROPE_DIM))
    ang = jnp.arange(seq, dtype=F32)[:, None] * inv[None, :]
    cos, sin = jnp.cos(ang), jnp.sin(ang)
    rope_tab = jnp.concatenate([cos, cos, -sin, sin], axis=1)

    h = _rmsnorm(x_parts, norm_mix_gain)
    zs = _matmul(h, w_small, tm=1024, tn=IN_PROJ_SMALL_TN, out_dtype=F32, name="in_proj_small")
    gates, w_up_bf16 = _matmul(h, w_gate, tm=1024, tn=1024, out_dtype=BF16, gate=True,
                               side=(w_up, SIDE_CAST_ROWS_UP), name="in_proj_gates")

    a_out = _pool_mixer(zs, pool_w.astype(BF16), pool_scale, seq=seq)

    q_scale = math.log2(math.e) / math.sqrt(QK_DIM)
    q = _q_proj(zs, off_cq // q_lora, q_a_norm_gain.reshape(1, q_lora), wq, qgn, qgr, rope_tab,
                n_seq=n_seq, seq=seq, out_scale=q_scale)
    k, vt = _kv_proj(zs, off_ckv // kv_lora, off_kr // (2 * ROPE_DIM), kv_a_norm_gain.reshape(1, kv_lora),
                     wk, wvt, kgn, kgr, rope_tab, n_seq=n_seq, seq=seq, tm=ATTN_KEY_TILE)
    b_out = _attention(q, k, vt).reshape(n_seq * seq, N_HEADS * V_DIM)

    merged = _gated_merge(a_out, b_out, w_branch_pool.astype(BF16), w_branch_mla.astype(BF16), gates)
    x1, w_down_bf16 = _matmul_residual(merged, w_o.astype(BF16), x_parts, side=(w_down, SIDE_CAST_ROWS_DOWN),
                                       tm=1024, tn=512, name="out_proj")

    act = _ffn_up(x1, norm_ffn_gain, w_up_bf16, conv_w, conv_b.reshape(1, -1), seq=seq)
    return _matmul_residual(act, w_down_bf16, (x1,), out_rows=tuple(p.shape[0] for p in x_parts),
                            tm=512, tn=512, name="down_proj")


def kernel(x_prompt, x_sample, norm_mix_gain, w_in, pool_w, pool_scale, q_a_norm_gain, w_uq,
           kv_a_norm_gain, w_ukv, q_norm_gain, k_norm_gain, w_branch_pool, w_branch_mla, w_o,
           norm_ffn_gain, w_up, conv_w, conv_b, w_down):
    bp, seq, d = x_prompt.shape
    bs = x_sample.shape[0]
    assert x_sample.shape[1:] == (seq, d)
    n_seq = bp + bs
    y = (x_prompt.reshape(bp * seq, d), x_sample.reshape(bs * seq, d))
    for l in range(norm_mix_gain.shape[0]):
        y = _layer(y, n_seq, seq, norm_mix_gain[l], w_in[l], pool_w[l], pool_scale[l], q_a_norm_gain[l],
                   w_uq[l], kv_a_norm_gain[l], w_ukv[l], q_norm_gain[l], k_norm_gain[l],
                   w_branch_pool[l], w_branch_mla[l], w_o[l], norm_ffn_gain[l], w_up[l],
                   conv_w[l], conv_b[l], w_down[l])
    return (y[0].reshape(bp, seq, d), y[1].reshape(bs, seq, d))
```
